```python
import jax
import jax.numpy as jnp
from jax import lax
import numpy as np

D_MODEL = 1024
BATCH = 8
SEQ = 8192
DEPTH = 4

HEAD_DIM = 64
HEADS_PER_MIXER = 4
MIX_WIDTH = HEADS_PER_MIXER * HEAD_DIM
N_MIXERS = 4
Q_BLOCK = 128
SUPER_Q = 512
SUFFIX_CHUNK = 128
ROPE_THETA = 500000.0
ROPE_DIMS = HEAD_DIM // 4
CMP_LEN = 32
CMP_STRIDE = 16
SEL_BLOCK = 64
SEL_TOPN = 16
WINDOW = 512
FORCE_BONUS = 1.0e6
IDX_HEADS = 4
IDX_DIM = 64
DSA_TOPK_MAX = 256
N_EXPERTS = 64
TOP_K = 8
N_GROUPS = 8
TOPK_GROUPS = 4
D_EXPERT = 256
ROUTE_SCALE = 2.5
MOE_BLOCK = 512
DEEPNORM_ALPHA = (2 * DEPTH) ** 0.25
DEEPNORM_BETA = (8 * DEPTH) ** -0.25
LN_EPS = 1e-5
NEG_INF = -1e30
TINY = 1e-30

IN_LAYOUT = (
    ('a_q', MIX_WIDTH), ('a_k_cmp', HEAD_DIM), ('a_v_cmp', HEAD_DIM), ('a_k_sel', HEAD_DIM),
    ('a_v_sel', HEAD_DIM), ('a_k_win', HEAD_DIM), ('a_v_win', HEAD_DIM), ('a_gate', 3 * HEADS_PER_MIXER),
    ('b_q', MIX_WIDTH), ('b_k', MIX_WIDTH), ('b_v', MIX_WIDTH),
    ('c_q', MIX_WIDTH), ('c_k', MIX_WIDTH), ('c_v', MIX_WIDTH), ('c_f', HEADS_PER_MIXER),
    ('d_q', MIX_WIDTH), ('d_k', HEAD_DIM), ('d_v', HEAD_DIM),
    ('d_iq', IDX_HEADS * IDX_DIM), ('d_ik', IDX_DIM), ('d_iw', IDX_HEADS),
)
IN_NAMES = tuple(n for n, _ in IN_LAYOUT)
IN_SIZES = tuple(c for _, c in IN_LAYOUT)
IN_WIDTH = sum(IN_SIZES)
IN_SPLITS = tuple(sum(IN_SIZES[:i + 1]) for i in range(len(IN_SIZES) - 1))

kernel_name = 'hybrid_nsa_stickbreak_fox_dsa_moe'


def _layer_norm(x, g, b):
    xf = x.astype(jnp.float32)
    mu = jnp.mean(xf, axis=-1, keepdims=True)
    var = jnp.mean(jnp.square(xf - mu), axis=-1, keepdims=True)
    return ((xf - mu) * lax.rsqrt(var + LN_EPS)).astype(x.dtype) * g + b


def _rope(x, positions):
    half = ROPE_DIMS // 2
    inv_freq = jnp.float32(ROPE_THETA) ** (-jnp.arange(half, dtype=jnp.float32) / half)
    ang = positions.astype(jnp.float32)[..., None] * inv_freq
    shape = ang.shape[:2] + (1,) * (x.ndim - 3) + (half,)
    cos = jnp.cos(ang).reshape(shape).astype(x.dtype)
    sin = jnp.sin(ang).reshape(shape).astype(x.dtype)
    x1, x2, rest = x[..., :half], x[..., half:ROPE_DIMS], x[..., ROPE_DIMS:]
    return jnp.concatenate([x1 * cos - x2 * sin, x1 * sin + x2 * cos, rest], axis=-1)


def _q_slice(t, q0):
    return lax.dynamic_slice_in_dim(t, q0, Q_BLOCK, axis=1)


def _unblock(out):
    nq, b, q = out.shape[:3]
    return jnp.moveaxis(out, 0, 1).reshape((b, nq * q) + out.shape[3:])


def _causal_sweep(block_fn, seq):
    outs = []
    for s0 in range(0, seq, SUPER_Q):
        n_b = min(SUPER_Q, seq - s0) // Q_BLOCK
        kl = s0 + n_b * Q_BLOCK
        outs.append(_unblock(lax.map(lambda i, s0=s0, kl=kl: block_fn(s0 + i * Q_BLOCK, kl),
                                     jnp.arange(n_b))))
    return jnp.concatenate(outs, axis=1)


def _suffix_sum(t):
    n_c = t.shape[-1] // SUFFIX_CHUNK
    tc = t.reshape(t.shape[:-1] + (n_c, SUFFIX_CHUNK))
    tri = jnp.asarray(np.tril(np.ones((SUFFIX_CHUNK, SUFFIX_CHUNK), np.float32)), dtype=t.dtype)
    within = jnp.einsum('...cj,ji->...ci', tc, tri)
    later_m = jnp.asarray(np.tril(np.ones((n_c, n_c), np.float32), -1), dtype=t.dtype)
    later = jnp.einsum('...c,cd->...d', within[..., 0], later_m)
    return (within + later[..., None]).reshape(t.shape)


def _masked_probs(s, mask):
    s = jnp.where(mask, s, NEG_INF)
    p = jnp.where(mask, jnp.exp(s - jnp.max(s, axis=-1, keepdims=True)), 0.0)
    return p / jnp.maximum(jnp.sum(p, axis=-1, keepdims=True), TINY)


_gather_rows = jax.vmap(lambda t, idx: t[idx])


def _nsa_attention(q, k_cmp_raw, v_cmp_raw, k_sel, v_sel, k_win, v_win, gates, pe_cmp, w_cmp1, w_cmp2):
    b_, s_, h_, d_ = q.shape
    scale = d_ ** -0.5
    n_cmp = (s_ - CMP_LEN) // CMP_STRIDE + 1
    n_sel = s_ // SEL_BLOCK
    top_n = min(SEL_TOPN, n_sel)
    cidx = np.arange(n_cmp)[:, None] * CMP_STRIDE + np.arange(CMP_LEN)[None, :]
    cmp_end = jnp.asarray(cidx[:, -1])
    sel_start = np.arange(n_sel) * SEL_BLOCK
    overlap = jnp.asarray(((cidx[:, :1] < sel_start[None, :] + SEL_BLOCK)
                           & (cidx[:, -1:] >= sel_start[None, :])).astype(np.float32))
    sel_ids = jnp.arange(n_sel)

    def compress(t, j):
        blocks = t[:, cidx] + pe_cmp[j]
        hid = jax.nn.gelu(blocks.reshape(b_, n_cmp, CMP_LEN * d_) @ w_cmp1[j])
        return hid @ w_cmp2[j]

    k_cmp = compress(k_cmp_raw, 0)
    v_cmp = compress(v_cmp_raw, 1)
    k_blk = k_sel.reshape(b_, n_sel, SEL_BLOCK, d_)
    v_blk = v_sel.reshape(b_, n_sel, SEL_BLOCK, d_)
    pad = ((0, 0), (WINDOW, 0), (0, 0))
    k_pad = jnp.pad(k_win, pad)
    v_pad = jnp.pad(v_win, pad)

    def block(i):
        q0 = i * Q_BLOCK
        qpos = q0 + jnp.arange(Q_BLOCK)
        qb = _q_slice(q, q0)
        gb = _q_slice(gates, q0)
        s_c = jnp.einsum('bqhd,bnd->bhqn', qb, k_cmp).astype(jnp.float32) * scale
        p_c = _masked_probs(s_c, cmp_end[None, :] <= qpos[:, None])
        o_c = jnp.einsum('bhqn,bnd->bqhd', p_c.astype(v_cmp.dtype), v_cmp)
        imp = jnp.einsum('bhqn,ns->bqs', p_c, overlap)
        q_blk = qpos[:, None] // SEL_BLOCK
        forced = (sel_ids[None, :] == 0) | (sel_ids[None, :] == q_blk) | (sel_ids[None, :] == q_blk - 1)
        imp = jnp.where(sel_ids[None, :] <= q_blk, imp + jnp.where(forced, FORCE_BONUS, 0.0), NEG_INF)
        _, top = lax.top_k(imp, top_n)
        kg = _gather_rows(k_blk, top).reshape(b_, Q_BLOCK, top_n * SEL_BLOCK, d_)
        vg = _gather_rows(v_blk, top).reshape(b_, Q_BLOCK, top_n * SEL_BLOCK, d_)
        tok = (top[..., None] * SEL_BLOCK + jnp.arange(SEL_BLOCK)).reshape(b_, Q_BLOCK, top_n * SEL_BLOCK)
        s_s = jnp.einsum('bqhd,bqkd->bhqk', qb, kg).astype(jnp.float32) * scale
        p_s = _masked_probs(s_s, (tok <= qpos[None, :, None])[:, None])
        o_s = jnp.einsum('bhqk,bqkd->bqhd', p_s.astype(vg.dtype), vg)
        kw = lax.dynamic_slice_in_dim(k_pad, q0, Q_BLOCK + WINDOW, axis=1)
        vw = lax.dynamic_slice_in_dim(v_pad, q0, Q_BLOCK + WINDOW, axis=1)
        wpos = q0 - WINDOW + jnp.arange(Q_BLOCK + WINDOW)
        m_w = (wpos[None, :] >= 0) & (wpos[None, :] <= qpos[:, None]) & (wpos[None, :] > qpos[:, None] - WINDOW)
        s_w = jnp.einsum('bqhd,bkd->bhqk', qb, kw).astype(jnp.float32) * scale
        p_w = _masked_probs(s_w, m_w)
        o_w = jnp.einsum('bhqk,bkd->bqhd', p_w.astype(vw.dtype), vw)
        return gb[..., 0:1] * o_c + gb[..., 1:2] * o_s + gb[..., 2:3] * o_w

    return _unblock(lax.map(block, jnp.arange(s_ // Q_BLOCK)))


def _stick_breaking_attention(q, k, v):
    scale = q.shape[-1] ** -0.5

    def block(q0, kl):
        qpos = q0 + jnp.arange(Q_BLOCK)
        mask = jnp.arange(kl)[None, :] < qpos[:, None]
        z = jnp.einsum('bqhd,bkhd->bhqk', _q_slice(q, q0), k[:, :kl]).astype(jnp.float32) * scale
        log_fail = jnp.where(mask, jax.nn.log_sigmoid(-z), 0.0)
        a = jnp.where(mask, jnp.exp(z + _suffix_sum(log_fail)), 0.0)
        return jnp.einsum('bhqk,bkhd->bqhd', a.astype(v.dtype), v[:, :kl])

    return _causal_sweep(block, q.shape[1])


def _forgetting_attention(q, k, v, log_f):
    scale = q.shape[-1] ** -0.5
    cum = jnp.moveaxis(jnp.cumsum(log_f, axis=1), 1, 2)

    def block(q0, kl):
        qpos = q0 + jnp.arange(Q_BLOCK)
        mask = jnp.arange(kl)[None, :] <= qpos[:, None]
        cq = lax.dynamic_slice_in_dim(cum, q0, Q_BLOCK, axis=2)
        s = (jnp.einsum('bqhd,bkhd->bhqk', _q_slice(q, q0), k[:, :kl]).astype(jnp.float32) * scale
             + (cq[:, :, :, None] - cum[:, :, None, :kl]))
        s = jnp.where(mask, s, NEG_INF)
        p = jnp.exp(s - jnp.max(s, axis=-1, keepdims=True))
        o = jnp.einsum('bhqk,bkhd->bqhd', p.astype(v.dtype), v[:, :kl])
        denom = jnp.moveaxis(jnp.sum(p, axis=-1), 1, 2)[..., None]
        return o / denom.astype(o.dtype)

    return _causal_sweep(block, q.shape[1])


def _dsa_attention(q, k, v, iq, ik, iw):
    s_ = q.shape[1]
    scale = q.shape[-1] ** -0.5
    idx_scale = (IDX_DIM ** -0.5) * (IDX_HEADS ** -0.5)
    topk = min(DSA_TOPK_MAX, s_ // 4)

    def block(q0, kl):
        qpos = q0 + jnp.arange(Q_BLOCK)
        rel = jax.nn.relu(jnp.einsum('bqhd,bkd->bqhk', _q_slice(iq, q0), ik[:, :kl])).astype(jnp.float32)
        score = jnp.einsum('bqh,bqhk->bqk', _q_slice(iw, q0).astype(jnp.float32), rel) * idx_scale
        score = jnp.where(jnp.arange(kl)[None, :] <= qpos[:, None], score, NEG_INF)
        _, sel = lax.top_k(score, topk)
        kg = _gather_rows(k, sel)
        vg = _gather_rows(v, sel)
        s = jnp.einsum('bqhd,bqkd->bhqk', _q_slice(q, q0), kg).astype(jnp.float32) * scale
        p = _masked_probs(s, (sel <= qpos[None, :, None])[:, None])
        return jnp.einsum('bhqk,bqkd->bqhd', p.astype(vg.dtype), vg)

    return _causal_sweep(block, s_)


def _token_mixer(x, positions, w_in, b_f, pe_cmp, w_cmp1, w_cmp2, w_gate, w_br, w_o):
    b_, s_, _ = x.shape
    p = dict(zip(IN_NAMES, jnp.split(x @ w_in, IN_SPLITS, axis=-1)))
    hd = lambda t: t.reshape(b_, s_, -1, HEAD_DIM)
    o_a = _nsa_attention(
        _rope(hd(p['a_q']), positions), p['a_k_cmp'], p['a_v_cmp'],
        _rope(p['a_k_sel'], positions), p['a_v_sel'], _rope(p['a_k_win'], positions), p['a_v_win'],
        jax.nn.sigmoid(p['a_gate'].reshape(b_, s_, HEADS_PER_MIXER, 3)), pe_cmp, w_cmp1, w_cmp2)
    o_b = _stick_breaking_attention(hd(p['b_q']), hd(p['b_k']), hd(p['b_v']))
    log_f = jax.nn.log_sigmoid((p['c_f'] + b_f).astype(jnp.float32))
    o_c = _forgetting_attention(hd(p['c_q']), hd(p['c_k']), hd(p['c_v']), log_f)
    o_d = _dsa_attention(
        _rope(hd(p['d_q']), positions), _rope(p['d_k'], positions), p['d_v'],
        _rope(p['d_iq'].reshape(b_, s_, IDX_HEADS, IDX_DIM), positions), _rope(p['d_ik'], positions), p['d_iw'])
    merged = sum(jax.nn.sigmoid(x @ w_gate[m]) * (o.reshape(b_, s_, MIX_WIDTH) @ w_br[m])
                 for m, o in enumerate((o_a, o_b, o_c, o_d)))
    return merged @ w_o


def _moe(x, w_router, router_bias, w_exp_gate, w_exp_up, w_exp_down, w_sh_gate, w_sh_up, w_sh_down):
    b_, s_, d_ = x.shape
    n_tok = b_ * s_
    xt = x.reshape(n_tok, d_)
    scores = jax.nn.sigmoid((xt @ w_router).astype(jnp.float32))
    biased = scores + router_bias.astype(jnp.float32)
    grp = biased.reshape(n_tok, N_GROUPS, N_EXPERTS // N_GROUPS)
    grp_score = jnp.sum(lax.top_k(grp, 2)[0], axis=-1)
    _, top_g = lax.top_k(grp_score, TOPK_GROUPS)
    gmask = jnp.any(top_g[..., :, None] == jnp.arange(N_GROUPS), axis=-2)
    emask = jnp.repeat(gmask, N_EXPERTS // N_GROUPS, axis=-1)
    _, eidx = lax.top_k(jnp.where(emask, biased, NEG_INF), TOP_K)
    w = jnp.take_along_axis(scores, eidx, axis=-1)
    w = w / jnp.sum(w, axis=-1, keepdims=True) * ROUTE_SCALE
    n_assign = n_tok * TOP_K
    flat_e = eidx.reshape(n_assign)
    flat_tok = jnp.repeat(jnp.arange(n_tok, dtype=jnp.int32), TOP_K)
    order = jnp.argsort(flat_e)
    sorted_e = flat_e[order]
    counts = jnp.bincount(flat_e, length=N_EXPERTS)
    padded = (counts + MOE_BLOCK - 1) // MOE_BLOCK * MOE_BLOCK
    pad_end = jnp.cumsum(padded)
    start = jnp.cumsum(counts) - counts
    dest = ((pad_end - padded)[sorted_e] + (jnp.arange(n_assign) - start[sorted_e])).astype(jnp.int32)
    n_slots = n_assign + N_EXPERTS * MOE_BLOCK
    n_blk = n_slots // MOE_BLOCK
    slot_tok = jnp.full((n_slots,), n_tok, jnp.int32).at[dest].set(flat_tok[order])
    blk_expert = jnp.minimum(jnp.searchsorted(pad_end, jnp.arange(n_blk) * MOE_BLOCK, side='right'),
                             N_EXPERTS - 1)
    x_pad = jnp.concatenate([xt, jnp.zeros((1, d_), xt.dtype)], axis=0)
    xs = x_pad[slot_tok].reshape(n_blk, MOE_BLOCK, d_)

    def expert_block(args):
        xb, e = args
        hb = jax.nn.silu(xb @ w_exp_gate[e]) * (xb @ w_exp_up[e])
        return hb @ w_exp_down[e]

    y_slots = lax.map(expert_block, (xs, blk_expert)).reshape(n_slots, d_)
    slot_of = jnp.zeros((n_assign,), jnp.int32).at[order].set(dest)
    routed = jnp.einsum('tk,tkd->td', w.astype(x.dtype), y_slots[slot_of].reshape(n_tok, TOP_K, d_))
    shared = (jax.nn.silu(xt @ w_sh_gate) * (xt @ w_sh_up)) @ w_sh_down
    return (routed + shared).reshape(b_, s_, d_)


def setup_inputs(seed: int = 0) -> dict:
    key = jax.random.key(seed)
    ks = jax.random.split(key, 24)
    nrm = lambda k, shape, scale: jax.random.normal(k, shape, jnp.float32) * scale
    x = nrm(ks[0], (BATCH, SEQ, D_MODEL), 1.0)
    offset = jax.random.randint(ks[1], (BATCH, 1), 0, 1024, dtype=jnp.int32)
    positions = offset + jnp.arange(SEQ, dtype=jnp.int32)[None, :]
    return {
        'x': x,
        'positions': positions,
        'w_in': nrm(ks[2], (DEPTH, D_MODEL, IN_WIDTH), D_MODEL ** -0.5),
        'b_f': 3.0 + nrm(ks[3], (DEPTH, HEADS_PER_MIXER), 0.5),
        'pe_cmp': nrm(ks[4], (DEPTH, 2, CMP_LEN, HEAD_DIM), 0.02),
        'w_cmp1': nrm(ks[5], (DEPTH, 2, CMP_LEN * HEAD_DIM, HEAD_DIM), (CMP_LEN * HEAD_DIM) ** -0.5),
        'w_cmp2': nrm(ks[6], (DEPTH, 2, HEAD_DIM, HEAD_DIM), HEAD_DIM ** -0.5),
        'w_gate': nrm(ks[7], (DEPTH, N_MIXERS, D_MODEL, D_MODEL), D_MODEL ** -0.5),
        'w_br': nrm(ks[8], (DEPTH, N_MIXERS, MIX_WIDTH, D_MODEL), MIX_WIDTH ** -0.5),
        'w_o': nrm(ks[9], (DEPTH, D_MODEL, D_MODEL), D_MODEL ** -0.5 * DEEPNORM_BETA),
        'ln1_g': 1.0 + nrm(ks[10], (DEPTH, D_MODEL), 0.01),
        'ln1_b': nrm(ks[11], (DEPTH, D_MODEL), 0.01),
        'w_router': nrm(ks[12], (DEPTH, D_MODEL, N_EXPERTS), D_MODEL ** -0.5),
        'router_bias': nrm(ks[13], (DEPTH, N_EXPERTS), 0.01),
        'w_exp_gate': nrm(ks[14], (DEPTH, N_EXPERTS, D_MODEL, D_EXPERT), D_MODEL ** -0.5),
        'w_exp_up': nrm(ks[15], (DEPTH, N_EXPERTS, D_MODEL, D_EXPERT), D_MODEL ** -0.5),
        'w_exp_down': nrm(ks[16], (DEPTH, N_EXPERTS, D_EXPERT, D_MODEL), D_EXPERT ** -0.5 * DEEPNORM_BETA),
        'w_sh_gate': nrm(ks[17], (DEPTH, D_MODEL, D_EXPERT), D_MODEL ** -0.5),
        'w_sh_up': nrm(ks[18], (DEPTH, D_MODEL, D_EXPERT), D_MODEL ** -0.5),
        'w_sh_down': nrm(ks[19], (DEPTH, D_EXPERT, D_MODEL), D_EXPERT ** -0.5 * DEEPNORM_BETA),
        'ln2_g': 1.0 + nrm(ks[20], (DEPTH, D_MODEL), 0.01),
        'ln2_b': nrm(ks[21], (DEPTH, D_MODEL), 0.01),
    }


def reference(x, positions, w_in, b_f, pe_cmp, w_cmp1, w_cmp2, w_gate, w_br, w_o, ln1_g, ln1_b,
              w_router, router_bias, w_exp_gate, w_exp_up, w_exp_down, w_sh_gate, w_sh_up, w_sh_down,
              ln2_g, ln2_b):
    for l in range(DEPTH):
        mix = _token_mixer(x, positions, w_in[l], b_f[l], pe_cmp[l], w_cmp1[l], w_cmp2[l],
                           w_gate[l], w_br[l], w_o[l])
        x = _layer_norm(DEEPNORM_ALPHA * x + mix, ln1_g[l], ln1_b[l])
        ffn = _moe(x, w_router[l], router_bias[l], w_exp_gate[l], w_exp_up[l], w_exp_down[l],
                   w_sh_gate[l], w_sh_up[l], w_sh_down[l])
        x = _layer_norm(DEEPNORM_ALPHA * x + ffn, ln2_g[l], ln2_b[l])
    return x
```

```python
import functools

import jax
import jax.numpy as jnp
import numpy as np
from jax import lax
from jax.experimental import pallas as pl
from jax.experimental.pallas import tpu as pltpu

HEAD_DIM = 64
HEADS_PER_MIXER = 4
MIX_WIDTH = HEADS_PER_MIXER * HEAD_DIM
N_MIXERS = 4
Q_BLOCK = 128
SUPER_Q = 512
SUFFIX_CHUNK = 128
ROPE_THETA = 500000.0
ROPE_DIMS = HEAD_DIM // 4
CMP_LEN = 32
CMP_STRIDE = 16
SEL_BLOCK = 64
SEL_TOPN = 16
WINDOW = 512
FORCE_BONUS = 1.0e6
IDX_HEADS = 4
IDX_DIM = 64
DSA_TOPK_MAX = 256
N_EXPERTS = 64
TOP_K = 8
N_GROUPS = 8
TOPK_GROUPS = 4
ROUTE_SCALE = 2.5
MOE_BLOCK = 512
LN_EPS = 1e-5
NEG_INF = -1e30
TINY = 1e-30

IN_LAYOUT = (
    ('a_q', MIX_WIDTH), ('a_k_cmp', HEAD_DIM), ('a_v_cmp', HEAD_DIM), ('a_k_sel', HEAD_DIM),
    ('a_v_sel', HEAD_DIM), ('a_k_win', HEAD_DIM), ('a_v_win', HEAD_DIM), ('a_gate', 3 * HEADS_PER_MIXER),
    ('b_q', MIX_WIDTH), ('b_k', MIX_WIDTH), ('b_v', MIX_WIDTH),
    ('c_q', MIX_WIDTH), ('c_k', MIX_WIDTH), ('c_v', MIX_WIDTH), ('c_f', HEADS_PER_MIXER),
    ('d_q', MIX_WIDTH), ('d_k', HEAD_DIM), ('d_v', HEAD_DIM),
    ('d_iq', IDX_HEADS * IDX_DIM), ('d_ik', IDX_DIM), ('d_iw', IDX_HEADS),
)
IN_NAMES = tuple(n for n, _ in IN_LAYOUT)
IN_SIZES = tuple(c for _, c in IN_LAYOUT)
IN_WIDTH = sum(IN_SIZES)
IN_SPLITS = tuple(sum(IN_SIZES[:i + 1]) for i in range(len(IN_SIZES) - 1))

LANES = 128
VMEM_LIMIT = 56 * 1024 * 1024


def _round_up(n, m):
    return (n + m - 1) // m * m


def _mm_kernel(x_ref, w_ref, o_ref):
    o_ref[...] = jnp.dot(x_ref[...].astype(jnp.bfloat16), w_ref[...],
                         preferred_element_type=jnp.float32)


def _mm(x, w, tm=512):
    m, k = x.shape
    n = w.shape[1]
    n_pad = _round_up(n, LANES)
    wb = w.astype(jnp.bfloat16)
    if n_pad != n:
        wb = jnp.pad(wb, ((0, 0), (0, n_pad - n)))
    tm = min(tm, m)
    out = pl.pallas_call(
        _mm_kernel,
        grid=(m // tm,),
        in_specs=[pl.BlockSpec((tm, k), lambda i: (i, 0)),
                  pl.BlockSpec((k, n_pad), lambda i: (0, 0))],
        out_specs=pl.BlockSpec((tm, n_pad), lambda i: (i, 0)),
        out_shape=jax.ShapeDtypeStruct((m, n_pad), jnp.float32),
        compiler_params=pltpu.CompilerParams(dimension_semantics=("parallel",),
                                             vmem_limit_bytes=VMEM_LIMIT),
        name="dense_proj",
    )(x, wb)
    return out[:, :n] if n_pad != n else out


def _expert_kernel(be_ref, x_ref, wg_ref, wu_ref, wd_ref, o_ref):
    del be_ref
    xb = x_ref[...].astype(jnp.bfloat16)
    g = jnp.dot(xb, wg_ref[0].astype(jnp.bfloat16), preferred_element_type=jnp.float32)
    u = jnp.dot(xb, wu_ref[0].astype(jnp.bfloat16), preferred_element_type=jnp.float32)
    h = (g * jax.nn.sigmoid(g)) * u
    o_ref[...] = jnp.dot(h.astype(jnp.bfloat16), wd_ref[0].astype(jnp.bfloat16),
                         preferred_element_type=jnp.float32)


def _expert_blocks(xs, blk_expert, w_g, w_u, w_d):
    n_slots, d = xs.shape
    d_e = w_g.shape[-1]
    n_blk = n_slots // MOE_BLOCK
    return pl.pallas_call(
        _expert_kernel,
        grid_spec=pltpu.PrefetchScalarGridSpec(
            num_scalar_prefetch=1,
            grid=(n_blk,),
            in_specs=[pl.BlockSpec((MOE_BLOCK, d), lambda i, be: (i, 0)),
                      pl.BlockSpec((1, d, d_e), lambda i, be: (be[i], 0, 0)),
                      pl.BlockSpec((1, d, d_e), lambda i, be: (be[i], 0, 0)),
                      pl.BlockSpec((1, d_e, d), lambda i, be: (be[i], 0, 0))],
            out_specs=pl.BlockSpec((MOE_BLOCK, d), lambda i, be: (i, 0))),
        out_shape=jax.ShapeDtypeStruct((n_slots, d), jnp.float32),
        compiler_params=pltpu.CompilerParams(dimension_semantics=("arbitrary",),
                                             vmem_limit_bytes=VMEM_LIMIT),
        name="routed_experts",
    )(blk_expert.astype(jnp.int32), xs, w_g, w_u, w_d)


def _layer_norm(x, g, b):
    mu = jnp.mean(x, axis=-1, keepdims=True)
    var = jnp.mean(jnp.square(x - mu), axis=-1, keepdims=True)
    return ((x - mu) * lax.rsqrt(var + LN_EPS)) * g + b


def _rope(x, positions):
    half = ROPE_DIMS // 2
    inv_freq = jnp.float32(ROPE_THETA) ** (-jnp.arange(half, dtype=jnp.float32) / half)
    ang = positions.astype(jnp.float32)[..., None] * inv_freq
    shape = ang.shape[:2] + (1,) * (x.ndim - 3) + (half,)
    cos = jnp.cos(ang).reshape(shape).astype(x.dtype)
    sin = jnp.sin(ang).reshape(shape).astype(x.dtype)
    x1, x2, rest = x[..., :half], x[..., half:ROPE_DIMS], x[..., ROPE_DIMS:]
    return jnp.concatenate([x1 * cos - x2 * sin, x1 * sin + x2 * cos, rest], axis=-1)


def _q_slice(t, q0):
    return lax.dynamic_slice_in_dim(t, q0, Q_BLOCK, axis=1)


def _unblock(out):
    nq, b, q = out.shape[:3]
    return jnp.moveaxis(out, 0, 1).reshape((b, nq * q) + out.shape[3:])


def _causal_sweep(block_fn, seq):
    outs = []
    for s0 in range(0, seq, SUPER_Q):
        n_b = min(SUPER_Q, seq - s0) // Q_BLOCK
        kl = s0 + n_b * Q_BLOCK
        outs.append(_unblock(lax.map(lambda i, s0=s0, kl=kl: block_fn(s0 + i * Q_BLOCK, kl),
                                     jnp.arange(n_b))))
    return jnp.concatenate(outs, axis=1)


def _suffix_sum(t):
    n_c = t.shape[-1] // SUFFIX_CHUNK
    tc = t.reshape(t.shape[:-1] + (n_c, SUFFIX_CHUNK))
    tri = jnp.asarray(np.tril(np.ones((SUFFIX_CHUNK, SUFFIX_CHUNK), np.float32)), dtype=t.dtype)
    within = jnp.einsum('...cj,ji->...ci', tc, tri)
    later_m = jnp.asarray(np.tril(np.ones((n_c, n_c), np.float32), -1), dtype=t.dtype)
    later = jnp.einsum('...c,cd->...d', within[..., 0], later_m)
    return (within + later[..., None]).reshape(t.shape)


def _masked_probs(s, mask):
    s = jnp.where(mask, s, NEG_INF)
    p = jnp.where(mask, jnp.exp(s - jnp.max(s, axis=-1, keepdims=True)), 0.0)
    return p / jnp.maximum(jnp.sum(p, axis=-1, keepdims=True), TINY)


_gather_rows = jax.vmap(lambda t, idx: t[idx])


def _nsa_attention(q, k_cmp_raw, v_cmp_raw, k_sel, v_sel, k_win, v_win, gates, pe_cmp, w_cmp1, w_cmp2):
    b_, s_, h_, d_ = q.shape
    scale = d_ ** -0.5
    n_cmp = (s_ - CMP_LEN) // CMP_STRIDE + 1
    n_sel = s_ // SEL_BLOCK
    top_n = min(SEL_TOPN, n_sel)
    cidx = np.arange(n_cmp)[:, None] * CMP_STRIDE + np.arange(CMP_LEN)[None, :]
    cmp_end = jnp.asarray(cidx[:, -1])
    sel_start = np.arange(n_sel) * SEL_BLOCK
    overlap = jnp.asarray(((cidx[:, :1] < sel_start[None, :] + SEL_BLOCK)
                           & (cidx[:, -1:] >= sel_start[None, :])).astype(np.float32))
    sel_ids = jnp.arange(n_sel)

    def compress(t, j):
        blocks = t[:, cidx] + pe_cmp[j]
        hid = jax.nn.gelu(blocks.reshape(b_, n_cmp, CMP_LEN * d_) @ w_cmp1[j])
        return hid @ w_cmp2[j]

    k_cmp = compress(k_cmp_raw, 0)
    v_cmp = compress(v_cmp_raw, 1)
    k_blk = k_sel.reshape(b_, n_sel, SEL_BLOCK, d_)
    v_blk = v_sel.reshape(b_, n_sel, SEL_BLOCK, d_)
    pad = ((0, 0), (WINDOW, 0), (0, 0))
    k_pad = jnp.pad(k_win, pad)
    v_pad = jnp.pad(v_win, pad)

    def block(i):
        q0 = i * Q_BLOCK
        qpos = q0 + jnp.arange(Q_BLOCK)
        qb = _q_slice(q, q0)
        gb = _q_slice(gates, q0)
        s_c = jnp.einsum('bqhd,bnd->bhqn', qb, k_cmp).astype(jnp.float32) * scale
        p_c = _masked_probs(s_c, cmp_end[None, :] <= qpos[:, None])
        o_c = jnp.einsum('bhqn,bnd->bqhd', p_c.astype(v_cmp.dtype), v_cmp)
        imp = jnp.einsum('bhqn,ns->bqs', p_c, overlap)
        q_blk = qpos[:, None] // SEL_BLOCK
        forced = (sel_ids[None, :] == 0) | (sel_ids[None, :] == q_blk) | (sel_ids[None, :] == q_blk - 1)
        imp = jnp.where(sel_ids[None, :] <= q_blk, imp + jnp.where(forced, FORCE_BONUS, 0.0), NEG_INF)
        _, top = lax.top_k(imp, top_n)
        kg = _gather_rows(k_blk, top).reshape(b_, Q_BLOCK, top_n * SEL_BLOCK, d_)
        vg = _gather_rows(v_blk, top).reshape(b_, Q_BLOCK, top_n * SEL_BLOCK, d_)
        tok = (top[..., None] * SEL_BLOCK + jnp.arange(SEL_BLOCK)).reshape(b_, Q_BLOCK, top_n * SEL_BLOCK)
        s_s = jnp.einsum('bqhd,bqkd->bhqk', qb, kg).astype(jnp.float32) * scale
        p_s = _masked_probs(s_s, (tok <= qpos[None, :, None])[:, None])
        o_s = jnp.einsum('bhqk,bqkd->bqhd', p_s.astype(vg.dtype), vg)
        kw = lax.dynamic_slice_in_dim(k_pad, q0, Q_BLOCK + WINDOW, axis=1)
        vw = lax.dynamic_slice_in_dim(v_pad, q0, Q_BLOCK + WINDOW, axis=1)
        wpos = q0 - WINDOW + jnp.arange(Q_BLOCK + WINDOW)
        m_w = (wpos[None, :] >= 0) & (wpos[None, :] <= qpos[:, None]) & (wpos[None, :] > qpos[:, None] - WINDOW)
        s_w = jnp.einsum('bqhd,bkd->bhqk', qb, kw).astype(jnp.float32) * scale
        p_w = _masked_probs(s_w, m_w)
        o_w = jnp.einsum('bhqk,bkd->bqhd', p_w.astype(vw.dtype), vw)
        return gb[..., 0:1] * o_c + gb[..., 1:2] * o_s + gb[..., 2:3] * o_w

    return _unblock(lax.map(block, jnp.arange(s_ // Q_BLOCK)))


def _stick_breaking_attention(q, k, v):
    scale = q.shape[-1] ** -0.5

    def block(q0, kl):
        qpos = q0 + jnp.arange(Q_BLOCK)
        mask = jnp.arange(kl)[None, :] < qpos[:, None]
        z = jnp.einsum('bqhd,bkhd->bhqk', _q_slice(q, q0), k[:, :kl]).astype(jnp.float32) * scale
        log_fail = jnp.where(mask, jax.nn.log_sigmoid(-z), 0.0)
        a = jnp.where(mask, jnp.exp(z + _suffix_sum(log_fail)), 0.0)
        return jnp.einsum('bhqk,bkhd->bqhd', a.astype(v.dtype), v[:, :kl])

    return _causal_sweep(block, q.shape[1])


def _forgetting_attention(q, k, v, log_f):
    scale = q.shape[-1] ** -0.5
    cum = jnp.moveaxis(jnp.cumsum(log_f, axis=1), 1, 2)

    def block(q0, kl):
        qpos = q0 + jnp.arange(Q_BLOCK)
        mask = jnp.arange(kl)[None, :] <= qpos[:, None]
        cq = lax.dynamic_slice_in_dim(cum, q0, Q_BLOCK, axis=2)
        s = (jnp.einsum('bqhd,bkhd->bhqk', _q_slice(q, q0), k[:, :kl]).astype(jnp.float32) * scale
             + (cq[:, :, :, None] - cum[:, :, None, :kl]))
        s = jnp.where(mask, s, NEG_INF)
        p = jnp.exp(s - jnp.max(s, axis=-1, keepdims=True))
        o = jnp.einsum('bhqk,bkhd->bqhd', p.astype(v.dtype), v[:, :kl])
        denom = jnp.moveaxis(jnp.sum(p, axis=-1), 1, 2)[..., None]
        return o / denom.astype(o.dtype)

    return _causal_sweep(block, q.shape[1])


def _dsa_attention(q, k, v, iq, ik, iw):
    s_ = q.shape[1]
    scale = q.shape[-1] ** -0.5
    idx_scale = (IDX_DIM ** -0.5) * (IDX_HEADS ** -0.5)
    topk = min(DSA_TOPK_MAX, s_ // 4)

    def block(q0, kl):
        qpos = q0 + jnp.arange(Q_BLOCK)
        rel = jax.nn.relu(jnp.einsum('bqhd,bkd->bqhk', _q_slice(iq, q0), ik[:, :kl])).astype(jnp.float32)
        score = jnp.einsum('bqh,bqhk->bqk', _q_slice(iw, q0).astype(jnp.float32), rel) * idx_scale
        score = jnp.where(jnp.arange(kl)[None, :] <= qpos[:, None], score, NEG_INF)
        _, sel = lax.top_k(score, topk)
        kg = _gather_rows(k, sel)
        vg = _gather_rows(v, sel)
        s = jnp.einsum('bqhd,bqkd->bhqk', _q_slice(q, q0), kg).astype(jnp.float32) * scale
        p = _masked_probs(s, (sel <= qpos[None, :, None])[:, None])
        return jnp.einsum('bhqk,bqkd->bqhd', p.astype(vg.dtype), vg)

    return _causal_sweep(block, s_)


def _token_mixer(x, positions, w_in, b_f, pe_cmp, w_cmp1, w_cmp2, w_gate, w_br, w_o):
    b_, s_, d_ = x.shape
    xt = x.reshape(b_ * s_, d_)
    proj = _mm(xt, w_in).reshape(b_, s_, IN_WIDTH)
    p = dict(zip(IN_NAMES, jnp.split(proj, IN_SPLITS, axis=-1)))
    hd = lambda t: t.reshape(b_, s_, -1, HEAD_DIM)
    o_a = _nsa_attention(
        _rope(hd(p['a_q']), positions), p['a_k_cmp'], p['a_v_cmp'],
        _rope(p['a_k_sel'], positions), p['a_v_sel'], _rope(p['a_k_win'], positions), p['a_v_win'],
        jax.nn.sigmoid(p['a_gate'].reshape(b_, s_, HEADS_PER_MIXER, 3)), pe_cmp, w_cmp1, w_cmp2)
    o_b = _stick_breaking_attention(hd(p['b_q']), hd(p['b_k']), hd(p['b_v']))
    log_f = jax.nn.log_sigmoid((p['c_f'] + b_f).astype(jnp.float32))
    o_c = _forgetting_attention(hd(p['c_q']), hd(p['c_k']), hd(p['c_v']), log_f)
    o_d = _dsa_attention(
        _rope(hd(p['d_q']), positions), _rope(p['d_k'], positions), p['d_v'],
        _rope(p['d_iq'].reshape(b_, s_, IDX_HEADS, IDX_DIM), positions), _rope(p['d_ik'], positions), p['d_iw'])
    merged = sum(jax.nn.sigmoid(_mm(xt, w_gate[m])) * _mm(o.reshape(b_ * s_, MIX_WIDTH), w_br[m])
                 for m, o in enumerate((o_a, o_b, o_c, o_d)))
    return _mm(merged, w_o).reshape(b_, s_, d_)


def _moe(x, w_router, router_bias, w_exp_gate, w_exp_up, w_exp_down, w_sh_gate, w_sh_up, w_sh_down):
    b_, s_, d_ = x.shape
    n_tok = b_ * s_
    xt = x.reshape(n_tok, d_)
    scores = jax.nn.sigmoid(_mm(xt, w_router))
    biased = scores + router_bias.astype(jnp.float32)
    grp = biased.reshape(n_tok, N_GROUPS, N_EXPERTS // N_GROUPS)
    grp_score = jnp.sum(lax.top_k(grp, 2)[0], axis=-1)
    _, top_g = lax.top_k(grp_score, TOPK_GROUPS)
    gmask = jnp.any(top_g[..., :, None] == jnp.arange(N_GROUPS), axis=-2)
    emask = jnp.repeat(gmask, N_EXPERTS // N_GROUPS, axis=-1)
    _, eidx = lax.top_k(jnp.where(emask, biased, NEG_INF), TOP_K)
    w = jnp.take_along_axis(scores, eidx, axis=-1)
    w = w / jnp.sum(w, axis=-1, keepdims=True) * ROUTE_SCALE
    n_assign = n_tok * TOP_K
    flat_e = eidx.reshape(n_assign)
    flat_tok = jnp.repeat(jnp.arange(n_tok, dtype=jnp.int32), TOP_K)
    order = jnp.argsort(flat_e)
    sorted_e = flat_e[order]
    counts = jnp.bincount(flat_e, length=N_EXPERTS)
    padded = (counts + MOE_BLOCK - 1) // MOE_BLOCK * MOE_BLOCK
    pad_end = jnp.cumsum(padded)
    start = jnp.cumsum(counts) - counts
    dest = ((pad_end - padded)[sorted_e] + (jnp.arange(n_assign) - start[sorted_e])).astype(jnp.int32)
    n_slots = n_assign + N_EXPERTS * MOE_BLOCK
    n_blk = n_slots // MOE_BLOCK
    slot_tok = jnp.full((n_slots,), n_tok, jnp.int32).at[dest].set(flat_tok[order])
    blk_expert = jnp.minimum(jnp.searchsorted(pad_end, jnp.arange(n_blk) * MOE_BLOCK, side='right'),
                             N_EXPERTS - 1)
    x_pad = jnp.concatenate([xt, jnp.zeros((1, d_), xt.dtype)], axis=0)
    xs = x_pad[slot_tok]
    y_slots = _expert_blocks(xs, blk_expert, w_exp_gate, w_exp_up, w_exp_down)
    slot_of = jnp.zeros((n_assign,), jnp.int32).at[order].set(dest)
    routed = jnp.einsum('tk,tkd->td', w, y_slots[slot_of].reshape(n_tok, TOP_K, d_))
    sh = _mm(xt, jnp.concatenate([w_sh_gate, w_sh_up], axis=1))
    d_e = w_sh_gate.shape[1]
    shared = _mm(jax.nn.silu(sh[:, :d_e]) * sh[:, d_e:], w_sh_down)
    return (routed + shared).reshape(b_, s_, d_)


def kernel(x, positions, w_in, b_f, pe_cmp, w_cmp1, w_cmp2, w_gate, w_br, w_o, ln1_g, ln1_b, w_router, router_bias, w_exp_gate, w_exp_up, w_exp_down, w_sh_gate, w_sh_up, w_sh_down, ln2_g, ln2_b):
    depth = w_in.shape[0]
    alpha = (2 * depth) ** 0.25
    for l in range(depth):
        mix = _token_mixer(x, positions, w_in[l], b_f[l], pe_cmp[l], w_cmp1[l], w_cmp2[l],
                           w_gate[l], w_br[l], w_o[l])
        x = _layer_norm(alpha * x + mix, ln1_g[l], ln1_b[l])
        ffn = _moe(x, w_router[l], router_bias[l], w_exp_gate[l], w_exp_up[l], w_exp_down[l],
                   w_sh_gate[l], w_sh_up[l], w_sh_down[l])
        x = _layer_norm(alpha * x + ffn, ln2_g[l], ln2_b[l])
    return x
```

```python
import functools
import struct

import jax
import jax.numpy as jnp
import numpy as np
from jax import lax
from jax.experimental import pallas as pl
from jax.experimental.pallas import tpu as pltpu

HEAD_DIM = 64
HEADS_PER_MIXER = 4
MIX_WIDTH = HEADS_PER_MIXER * HEAD_DIM
N_MIXERS = 4
ROPE_THETA = 500000.0
ROPE_DIMS = HEAD_DIM // 4
CMP_LEN = 32
CMP_STRIDE = 16
SEL_BLOCK = 64
SEL_TOPN = 16
WINDOW = 512
FORCE_BONUS = 1.0e6
IDX_HEADS = 4
IDX_DIM = 64
DSA_TOPK_MAX = 256
N_EXPERTS = 64
TOP_K = 8
N_GROUPS = 8
TOPK_GROUPS = 4
ROUTE_SCALE = 2.5
MOE_BLOCK = 512
LN_EPS = 1e-5
NEG_INF = -1e30
TINY = 1e-30

IN_LAYOUT = (
    ('a_q', MIX_WIDTH), ('a_k_cmp', HEAD_DIM), ('a_v_cmp', HEAD_DIM), ('a_k_sel', HEAD_DIM),
    ('a_v_sel', HEAD_DIM), ('a_k_win', HEAD_DIM), ('a_v_win', HEAD_DIM), ('a_gate', 3 * HEADS_PER_MIXER),
    ('b_q', MIX_WIDTH), ('b_k', MIX_WIDTH), ('b_v', MIX_WIDTH),
    ('c_q', MIX_WIDTH), ('c_k', MIX_WIDTH), ('c_v', MIX_WIDTH), ('c_f', HEADS_PER_MIXER),
    ('d_q', MIX_WIDTH), ('d_k', HEAD_DIM), ('d_v', HEAD_DIM),
    ('d_iq', IDX_HEADS * IDX_DIM), ('d_ik', IDX_DIM), ('d_iw', IDX_HEADS),
)
IN_NAMES = tuple(n for n, _ in IN_LAYOUT)
IN_SIZES = tuple(c for _, c in IN_LAYOUT)
IN_WIDTH = sum(IN_SIZES)
IN_OFFSETS = dict(zip(IN_NAMES, np.cumsum((0,) + IN_SIZES[:-1]).tolist()))

LANES = 128
VMEM_LIMIT = 56 * 1024 * 1024
BF16 = jnp.bfloat16
F32 = jnp.float32
ATTN_SCALE = HEAD_DIM ** -0.5


def _round_up(n, m):
    return (n + m - 1) // m * m


def _f32_order_key(f):
    b = struct.unpack('<i', struct.pack('<f', f))[0]
    return b if b >= 0 else b ^ 0x7FFFFFFF


KEY_NEG_INF = _f32_order_key(NEG_INF)
INT_MIN = -2 ** 31


def _dot_nt(a, b):
    return lax.dot_general(a, b, (((1,), (1,)), ((), ())), preferred_element_type=F32)


def _params(*sem):
    return pltpu.CompilerParams(dimension_semantics=sem, vmem_limit_bytes=VMEM_LIMIT)


def _mm_kernel(x_ref, w_ref, o_ref):
    o_ref[...] = jnp.dot(x_ref[...].astype(BF16), w_ref[...], preferred_element_type=F32)


def _mm(x, w, tm=512):
    m, k = x.shape
    n = w.shape[1]
    n_pad = _round_up(n, LANES)
    wb = w.astype(BF16)
    if n_pad != n:
        wb = jnp.pad(wb, ((0, 0), (0, n_pad - n)))
    tm = min(tm, m)
    out = pl.pallas_call(
        _mm_kernel,
        grid=(m // tm,),
        in_specs=[pl.BlockSpec((tm, k), lambda i: (i, 0)),
                  pl.BlockSpec((k, n_pad), lambda i: (0, 0))],
        out_specs=pl.BlockSpec((tm, n_pad), lambda i: (i, 0)),
        out_shape=jax.ShapeDtypeStruct((m, n_pad), F32),
        compiler_params=_params("parallel"),
        name="dense_proj",
    )(x, wb)
    return out[:, :n] if n_pad != n else out


def _softmax_init(m_sc, l_sc, acc_sc):
    m_sc[...] = jnp.full(m_sc.shape, NEG_INF, F32)
    l_sc[...] = jnp.zeros(l_sc.shape, F32)
    acc_sc[...] = jnp.zeros(acc_sc.shape, F32)


def _softmax_step(s, mask, vt, m_sc, l_sc, acc_sc):
    if mask is not None:
        s = jnp.where(mask, s, NEG_INF)
    m_old = m_sc[...]
    m_new = jnp.maximum(m_old, jnp.max(s, axis=1, keepdims=True))
    p = jnp.exp(s - m_new)
    if mask is not None:
        p = jnp.where(mask, p, 0.0)
    alpha = jnp.exp(m_old - m_new)
    l_sc[...] = alpha * l_sc[...] + jnp.sum(p, axis=1, keepdims=True)
    acc_sc[...] = alpha * acc_sc[...] + jnp.dot(p.astype(BF16), vt, preferred_element_type=F32)
    m_sc[...] = m_new


def _tile_positions(q0, k0, tq, tk):
    qpos = q0 + lax.broadcasted_iota(jnp.int32, (tq, tk), 0)
    kpos = k0 + lax.broadcasted_iota(jnp.int32, (tq, tk), 1)
    return qpos, kpos


def _stack_mask(mask, heads):
    tq, tk = mask.shape
    return jnp.broadcast_to(mask[None], (heads, tq, tk)).reshape(heads * tq, tk)


def _fox_kernel(q_ref, k_ref, v_ref, cq_ref, ck_ref, o_ref, m_sc, l_sc, acc_sc, *, tq, tk):
    q0 = pl.program_id(2) * tq
    q = q_ref[0, 0]
    cq = cq_ref[0, 0]
    _softmax_init(m_sc, l_sc, acc_sc)

    def step(j, masked):
        k0 = pl.multiple_of(j * tk, tk)
        kt = k_ref[0, 0, pl.ds(k0, tk), :]
        vt = v_ref[0, 0, pl.ds(k0, tk), :]
        s = _dot_nt(q, kt) + (cq - ck_ref[0, 0, :, pl.ds(k0, tk)])
        mask = None
        if masked:
            qpos, kpos = _tile_positions(q0, k0, tq, tk)
            mask = kpos <= qpos
        _softmax_step(s, mask, vt, m_sc, l_sc, acc_sc)

    n_full = q0 // tk

    def body(j, c):
        step(j, False)
        return c

    lax.fori_loop(0, n_full, body, 0)
    for d in range(tq // tk):
        step(n_full + d, True)
    o_ref[0, 0] = acc_sc[...] / l_sc[...]


def _fox_attention(q, k, v, cum, tq=512, tk=512):
    b, h, s, d = q.shape
    tq, tk = min(tq, s), min(tk, s)
    kern = functools.partial(_fox_kernel, tq=tq, tk=tk)
    return pl.pallas_call(
        kern,
        grid=(b, h, s // tq),
        in_specs=[pl.BlockSpec((1, 1, tq, d), lambda i, j, t: (i, j, t, 0)),
                  pl.BlockSpec((1, 1, s, d), lambda i, j, t: (i, j, 0, 0)),
                  pl.BlockSpec((1, 1, s, d), lambda i, j, t: (i, j, 0, 0)),
                  pl.BlockSpec((1, 1, tq, 1), lambda i, j, t: (i, j, t, 0)),
                  pl.BlockSpec((1, 1, 1, s), lambda i, j, t: (i, j, 0, 0))],
        out_specs=pl.BlockSpec((1, 1, tq, d), lambda i, j, t: (i, j, t, 0)),
        out_shape=jax.ShapeDtypeStruct((b, h, s, d), F32),
        scratch_shapes=[pltpu.VMEM((tq, 1), F32), pltpu.VMEM((tq, 1), F32), pltpu.VMEM((tq, d), F32)],
        compiler_params=_params("parallel", "parallel", "arbitrary"),
        name="forgetting_attention",
    )(q, k, v, cum[..., None], cum[:, :, None, :])


def _sb_kernel(q_ref, k_ref, v_ref, tri_ref, o_ref, r_sc, acc_sc, *, tq, ck):
    q0 = pl.program_id(2) * tq
    q = q_ref[0, 0]
    tri = tri_ref[...]
    r_sc[...] = jnp.zeros(r_sc.shape, F32)
    acc_sc[...] = jnp.zeros(acc_sc.shape, F32)

    def chunk(c, masked):
        k0 = pl.multiple_of(c * ck, ck)
        kt = k_ref[0, 0, pl.ds(k0, ck), :]
        vt = v_ref[0, 0, pl.ds(k0, ck), :]
        z = _dot_nt(q, kt)
        log_fail = -(jnp.maximum(z, 0.0) + jnp.log1p(jnp.exp(-jnp.abs(z))))
        if masked:
            qpos, kpos = _tile_positions(q0, k0, tq, ck)
            mask = kpos < qpos
            log_fail = jnp.where(mask, log_fail, 0.0)
        within = jnp.dot(log_fail.astype(BF16), tri, preferred_element_type=F32)
        a = jnp.exp(z + within + r_sc[...])
        if masked:
            a = jnp.where(mask, a, 0.0)
        acc_sc[...] += jnp.dot(a.astype(BF16), vt, preferred_element_type=F32)
        r_sc[...] += within[:, 0:1]

    n_diag = tq // ck
    n_below = q0 // ck
    for d in reversed(range(n_diag)):
        chunk(n_below + d, True)

    def body(i, c):
        base = n_below - (i + 1) * n_diag
        for d in reversed(range(n_diag)):
            chunk(base + d, False)
        return c

    lax.fori_loop(0, n_below // n_diag, body, 0)
    o_ref[0, 0] = acc_sc[...]


def _stick_breaking_attention(q, k, v, tq=512, ck=128):
    b, h, s, d = q.shape
    tq = min(tq, s)
    tri = jnp.asarray(np.tril(np.ones((ck, ck), np.float32)), dtype=BF16)
    kern = functools.partial(_sb_kernel, tq=tq, ck=ck)
    return pl.pallas_call(
        kern,
        grid=(b, h, s // tq),
        in_specs=[pl.BlockSpec((1, 1, tq, d), lambda i, j, t: (i, j, t, 0)),
                  pl.BlockSpec((1, 1, s, d), lambda i, j, t: (i, j, 0, 0)),
                  pl.BlockSpec((1, 1, s, d), lambda i, j, t: (i, j, 0, 0)),
                  pl.BlockSpec((ck, ck), lambda i, j, t: (0, 0))],
        out_specs=pl.BlockSpec((1, 1, tq, d), lambda i, j, t: (i, j, t, 0)),
        out_shape=jax.ShapeDtypeStruct((b, h, s, d), F32),
        scratch_shapes=[pltpu.VMEM((tq, 1), F32), pltpu.VMEM((tq, d), F32)],
        compiler_params=_params("parallel", "parallel", "arbitrary"),
        name="stick_breaking_attention",
    )(q, k, v, tri)


def _dsa_kernel(iq_ref, iw_ref, ik_ref, q_ref, k_ref, v_ref, triu_ref, o_ref,
                key_sc, m_sc, l_sc, acc_sc, *, tq, tk, topk, heads):
    q0 = pl.program_id(1) * tq
    n_kt = (q0 + tq + tk - 1) // tk
    n_ch = n_kt * (tk // LANES)
    idx_scale = (IDX_DIM ** -0.5) * (IDX_HEADS ** -0.5)
    iw = iw_ref[0]

    def score_tile(j, c):
        k0 = pl.multiple_of(j * tk, tk)
        ikt = ik_ref[0, pl.ds(k0, tk), :]
        score = jnp.zeros((tq, tk), F32)
        for hh in range(IDX_HEADS):
            rel = jnp.maximum(_dot_nt(iq_ref[0, hh], ikt), 0.0)
            score = score + iw[:, hh:hh + 1] * rel
        score = score * idx_scale
        qpos, kpos = _tile_positions(q0, k0, tq, tk)
        score = jnp.where(score == 0.0, 0.0, score)
        score = jnp.where(kpos <= qpos, score, NEG_INF)
        bits = pltpu.bitcast(score, jnp.int32)
        key_sc[:, pl.ds(k0, tk)] = jnp.where(bits >= 0, bits, bits ^ 0x7FFFFFFF)
        return c

    lax.fori_loop(0, n_kt, score_tile, 0)

    def count(pred, n=n_ch):
        def body(c, acc):
            kc = key_sc[:, pl.ds(pl.multiple_of(c * LANES, LANES), LANES)]
            return acc + jnp.where(pred(kc), 1, 0)
        acc = lax.fori_loop(0, n, body, jnp.zeros((tq, LANES), jnp.int32))
        return jnp.sum(acc, axis=1, keepdims=True)

    c0 = count(lambda kc: kc >= 0)
    thr = jnp.where(c0 >= topk, 0, INT_MIN).astype(jnp.int32)

    def bit_step(i, thr):
        cand = thr | lax.shift_left(jnp.int32(1), 30 - i)
        c = count(lambda kc: kc >= cand)
        return jnp.where(c >= topk, cand, thr)

    thr = lax.fori_loop(0, 31, bit_step, thr)
    thr = jnp.maximum(thr, KEY_NEG_INF + 1)
    n_ge = count(lambda kc: kc >= thr)

    n_fix = jnp.where(jnp.max(n_ge) > topk, n_ch, 0)
    n_gt = count(lambda kc: kc > thr, n_fix)
    room = (topk - n_gt).astype(F32)

    def fix(c, seen):
        sl = pl.ds(pl.multiple_of(c * LANES, LANES), LANES)
        kc = key_sc[:, sl]
        eq = kc == thr
        eqf = jnp.where(eq, 1.0, 0.0)
        incl = jnp.dot(eqf.astype(BF16), triu_ref[...], preferred_element_type=F32) + seen
        drop = eq & (incl - eqf >= room)
        key_sc[:, sl] = jnp.where(drop, kc - 1, kc)
        return seen + jnp.sum(eqf, axis=1, keepdims=True)

    lax.fori_loop(0, n_fix, fix, jnp.zeros((tq, 1), F32))

    q = q_ref[0].reshape(heads * tq, HEAD_DIM)
    _softmax_init(m_sc, l_sc, acc_sc)

    def attn_tile(j, c):
        k0 = pl.multiple_of(j * tk, tk)
        kt = k_ref[0, pl.ds(k0, tk), :]
        vt = v_ref[0, pl.ds(k0, tk), :]
        mask = _stack_mask(key_sc[:, pl.ds(k0, tk)] >= thr, heads)
        _softmax_step(_dot_nt(q, kt), mask, vt, m_sc, l_sc, acc_sc)
        return c

    lax.fori_loop(0, n_kt, attn_tile, 0)
    o = acc_sc[...] / jnp.maximum(l_sc[...], TINY)
    o_ref[0] = o.reshape(heads, tq, HEAD_DIM)


def _dsa_attention(q, k, v, iq, ik, iw, tq=128, tk=512):
    b, h, s, d = q.shape
    tq, tk = min(tq, s), min(tk, s)
    topk = min(DSA_TOPK_MAX, s // 4)
    triu = jnp.asarray(np.triu(np.ones((LANES, LANES), np.float32)), dtype=BF16)
    kern = functools.partial(_dsa_kernel, tq=tq, tk=tk, topk=topk, heads=h)
    return pl.pallas_call(
        kern,
        grid=(b, s // tq),
        in_specs=[pl.BlockSpec((1, IDX_HEADS, tq, IDX_DIM), lambda i, t: (i, 0, t, 0)),
                  pl.BlockSpec((1, tq, IDX_HEADS), lambda i, t: (i, t, 0)),
                  pl.BlockSpec((1, s, IDX_DIM), lambda i, t: (i, 0, 0)),
                  pl.BlockSpec((1, h, tq, d), lambda i, t: (i, 0, t, 0)),
                  pl.BlockSpec((1, s, d), lambda i, t: (i, 0, 0)),
                  pl.BlockSpec((1, s, d), lambda i, t: (i, 0, 0)),
                  pl.BlockSpec((LANES, LANES), lambda i, t: (0, 0))],
        out_specs=pl.BlockSpec((1, h, tq, d), lambda i, t: (i, 0, t, 0)),
        out_shape=jax.ShapeDtypeStruct((b, h, s, d), F32),
        scratch_shapes=[pltpu.VMEM((tq, s), jnp.int32), pltpu.VMEM((h * tq, 1), F32),
                        pltpu.VMEM((h * tq, 1), F32), pltpu.VMEM((h * tq, d), F32)],
        compiler_params=_params("parallel", "arbitrary"),
        name="dsa_attention",
    )(iq, iw, ik, q, k, v, triu)


def _compress_kernel(x_ref, pe_ref, w1_ref, w2_ref, o_ref, *, half):
    x = x_ref[0, 0]
    n_rows = x.shape[0]
    ya = jnp.dot((x + pe_ref[0, :, :half]).astype(BF16), w1_ref[0, :half, :], preferred_element_type=F32)
    yb = jnp.dot((x + pe_ref[0, :, half:]).astype(BF16), w1_ref[0, half:, :], preferred_element_type=F32)
    hid = ya + pltpu.roll(yb, n_rows - 1, 0)
    o_ref[0, 0] = jnp.dot(jax.nn.gelu(hid).astype(BF16), w2_ref[0], preferred_element_type=F32)


def _compress(t2, pe_cmp, w_cmp1, w_cmp2):
    _, b, s, d = t2.shape
    assert CMP_LEN == 2 * CMP_STRIDE
    n_rows = s // CMP_STRIDE
    half = CMP_STRIDE * d
    x = t2.reshape(2, b, n_rows, half)
    pe = pe_cmp.reshape(2, 1, CMP_LEN * d)
    kern = functools.partial(_compress_kernel, half=half)
    return pl.pallas_call(
        kern,
        grid=(2, b),
        in_specs=[pl.BlockSpec((1, 1, n_rows, half), lambda j, i: (j, i, 0, 0)),
                  pl.BlockSpec((1, 1, CMP_LEN * d), lambda j, i: (j, 0, 0)),
                  pl.BlockSpec((1, CMP_LEN * d, d), lambda j, i: (j, 0, 0)),
                  pl.BlockSpec((1, d, d), lambda j, i: (j, 0, 0))],
        out_specs=pl.BlockSpec((1, 1, n_rows, d), lambda j, i: (j, i, 0, 0)),
        out_shape=jax.ShapeDtypeStruct((2, b, n_rows, d), F32),
        compiler_params=_params("parallel", "parallel"),
        name="nsa_compress",
    )(x, pe, w_cmp1.astype(BF16), w_cmp2.astype(BF16))


def _nsa_cmp_kernel(q_ref, kc_ref, vc_ref, ov_ref, o_ref, sel_ref, *, tq, heads, n_cmp, top_n):
    q0 = pl.program_id(1) * tq
    q = q_ref[0].reshape(heads * tq, HEAD_DIM)
    kc = kc_ref[0]
    n_rows = kc.shape[0]
    n_sel = sel_ref.shape[-1]
    s = _dot_nt(q, kc)
    qpos, cid = _tile_positions(q0, 0, tq, n_rows)
    mask = _stack_mask((cid * CMP_STRIDE + (CMP_LEN - 1) <= qpos) & (cid < n_cmp), heads)
    s = jnp.where(mask, s, NEG_INF)
    p = jnp.where(mask, jnp.exp(s - jnp.max(s, axis=1, keepdims=True)), 0.0)
    p = p / jnp.maximum(jnp.sum(p, axis=1, keepdims=True), TINY)
    o_ref[0] = jnp.dot(p.astype(BF16), vc_ref[0], preferred_element_type=F32).reshape(heads, tq, HEAD_DIM)

    p_sum = jnp.sum(p.reshape(heads, tq, n_rows), axis=0)
    p_hi = p_sum.astype(BF16)
    p_lo = (p_sum - p_hi.astype(F32)).astype(BF16)
    imp = (jnp.dot(p_hi, ov_ref[...], preferred_element_type=F32)
           + jnp.dot(p_lo, ov_ref[...], preferred_element_type=F32))
    qrow, sid = _tile_positions(q0, 0, tq, n_sel)
    q_blk = qrow // SEL_BLOCK
    forced = (sid == 0) | (sid == q_blk) | (sid == q_blk - 1)
    imp = jnp.where(sid <= q_blk, imp + jnp.where(forced, FORCE_BONUS, 0.0), NEG_INF)

    sidf = sid.astype(F32)
    picked = jnp.zeros((tq, n_sel), F32)
    for _ in range(top_n):
        best = jnp.max(imp, axis=1, keepdims=True)
        first = jnp.min(jnp.where(imp == best, sidf, float(n_sel)), axis=1, keepdims=True)
        hit = sidf == first
        picked = jnp.where(hit, 1.0, picked)
        imp = jnp.where(hit, -3.0e38, imp)
    sel_ref[0] = picked.astype(BF16)


def _nsa_cmp_select(q, k_cmp, v_cmp, tq=128):
    b, h, s, d = q.shape
    tq = min(tq, s)
    n_rows = k_cmp.shape[1]
    n_cmp = (s - CMP_LEN) // CMP_STRIDE + 1
    n_sel = s // SEL_BLOCK
    top_n = min(SEL_TOPN, n_sel)
    c_lo = np.arange(n_rows)[:, None] * CMP_STRIDE
    s_lo = np.arange(n_sel)[None, :] * SEL_BLOCK
    overlap = ((c_lo < s_lo + SEL_BLOCK) & (c_lo + CMP_LEN - 1 >= s_lo) & (np.arange(n_rows)[:, None] < n_cmp))
    overlap = jnp.asarray(overlap.astype(np.float32), dtype=BF16)
    kern = functools.partial(_nsa_cmp_kernel, tq=tq, heads=h, n_cmp=n_cmp, top_n=top_n)
    return pl.pallas_call(
        kern,
        grid=(b, s // tq),
        in_specs=[pl.BlockSpec((1, h, tq, d), lambda i, t: (i, 0, t, 0)),
                  pl.BlockSpec((1, n_rows, d), lambda i, t: (i, 0, 0)),
                  pl.BlockSpec((1, n_rows, d), lambda i, t: (i, 0, 0)),
                  pl.BlockSpec((n_rows, n_sel), lambda i, t: (0, 0))],
        out_specs=[pl.BlockSpec((1, h, tq, d), lambda i, t: (i, 0, t, 0)),
                   pl.BlockSpec((1, tq, n_sel), lambda i, t: (i, t, 0))],
        out_shape=[jax.ShapeDtypeStruct((b, h, s, d), F32), jax.ShapeDtypeStruct((b, s, n_sel), BF16)],
        compiler_params=_params("parallel", "parallel"),
        name="nsa_compressed_select",
    )(q, k_cmp, v_cmp, overlap)


def _nsa_sel_kernel(q_ref, sel_ref, ex_ref, k_ref, v_ref, o_ref, m_sc, l_sc, acc_sc, *, tq, tk, heads):
    q0 = pl.program_id(1) * tq
    n_kt = (q0 + tq + tk - 1) // tk
    q = q_ref[0].reshape(heads * tq, HEAD_DIM)
    picked = sel_ref[0]
    _softmax_init(m_sc, l_sc, acc_sc)

    def tile(j, c):
        k0 = pl.multiple_of(j * tk, tk)
        kt = k_ref[0, pl.ds(k0, tk), :]
        vt = v_ref[0, pl.ds(k0, tk), :]
        in_block = jnp.dot(picked, ex_ref[:, pl.ds(k0, tk)], preferred_element_type=F32)
        qpos, kpos = _tile_positions(q0, k0, tq, tk)
        mask = _stack_mask((in_block > 0.5) & (kpos <= qpos), heads)
        _softmax_step(_dot_nt(q, kt), mask, vt, m_sc, l_sc, acc_sc)
        return c

    lax.fori_loop(0, n_kt, tile, 0)
    o = acc_sc[...] / jnp.maximum(l_sc[...], TINY)
    o_ref[0] = o.reshape(heads, tq, HEAD_DIM)


def _nsa_selected_attention(q, picked, k, v, tq=128, tk=512):
    b, h, s, d = q.shape
    tq, tk = min(tq, s), min(tk, s)
    n_sel = picked.shape[-1]
    expand = (np.arange(n_sel)[:, None] == np.arange(s)[None, :] // SEL_BLOCK).astype(np.float32)
    expand = jnp.asarray(expand, dtype=BF16)
    kern = functools.partial(_nsa_sel_kernel, tq=tq, tk=tk, heads=h)
    return pl.pallas_call(
        kern,
        grid=(b, s // tq),
        in_specs=[pl.BlockSpec((1, h, tq, d), lambda i, t: (i, 0, t, 0)),
                  pl.BlockSpec((1, tq, n_sel), lambda i, t: (i, t, 0)),
                  pl.BlockSpec((n_sel, s), lambda i, t: (0, 0)),
                  pl.BlockSpec((1, s, d), lambda i, t: (i, 0, 0)),
                  pl.BlockSpec((1, s, d), lambda i, t: (i, 0, 0))],
        out_specs=pl.BlockSpec((1, h, tq, d), lambda i, t: (i, 0, t, 0)),
        out_shape=jax.ShapeDtypeStruct((b, h, s, d), F32),
        scratch_shapes=[pltpu.VMEM((h * tq, 1), F32), pltpu.VMEM((h * tq, 1), F32),
                        pltpu.VMEM((h * tq, d), F32)],
        compiler_params=_params("parallel", "arbitrary"),
        name="nsa_selected_attention",
    )(q, picked, expand, k, v)


def _nsa_win_kernel(q_ref, k_ref, v_ref, oc_ref, os_ref, g_ref, o_ref, m_sc, l_sc, acc_sc, *, tq, tk, heads):
    q0 = pl.program_id(1) * tq
    j_lo = jnp.maximum(q0 - WINDOW, 0) // tk
    j_hi = (q0 + tq) // tk
    q = q_ref[0].reshape(heads * tq, HEAD_DIM)
    _softmax_init(m_sc, l_sc, acc_sc)

    def tile(j, c):
        k0 = pl.multiple_of(j * tk, tk)
        kt = k_ref[0, pl.ds(k0, tk), :]
        vt = v_ref[0, pl.ds(k0, tk), :]
        qpos, kpos = _tile_positions(q0, k0, tq, tk)
        mask = _stack_mask((kpos <= qpos) & (kpos > qpos - WINDOW), heads)
        _softmax_step(_dot_nt(q, kt), mask, vt, m_sc, l_sc, acc_sc)
        return c

    lax.fori_loop(j_lo, j_hi, tile, 0)
    o_w = (acc_sc[...] / jnp.maximum(l_sc[...], TINY)).reshape(heads, tq, HEAD_DIM)
    g = g_ref[0]
    o_ref[0] = g[:, :, 0:1] * oc_ref[0] + g[:, :, 1:2] * os_ref[0] + g[:, :, 2:3] * o_w


def _nsa_window_merge(q, k, v, o_c, o_s, gates, tq=256, tk=256):
    b, h, s, d = q.shape
    tq, tk = min(tq, s), min(tk, s)
    assert tq % tk == 0 and WINDOW % tk == 0
    kern = functools.partial(_nsa_win_kernel, tq=tq, tk=tk, heads=h)
    head_spec = pl.BlockSpec((1, h, tq, d), lambda i, t: (i, 0, t, 0))
    return pl.pallas_call(
        kern,
        grid=(b, s // tq),
        in_specs=[head_spec,
                  pl.BlockSpec((1, s, d), lambda i, t: (i, 0, 0)),
                  pl.BlockSpec((1, s, d), lambda i, t: (i, 0, 0)),
                  head_spec, head_spec,
                  pl.BlockSpec((1, h, tq, 3), lambda i, t: (i, 0, t, 0))],
        out_specs=head_spec,
        out_shape=jax.ShapeDtypeStruct((b, h, s, d), F32),
        scratch_shapes=[pltpu.VMEM((h * tq, 1), F32), pltpu.VMEM((h * tq, 1), F32),
                        pltpu.VMEM((h * tq, d), F32)],
        compiler_params=_params("parallel", "parallel"),
        name="nsa_window_merge",
    )(q, k, v, o_c, o_s, gates)


def _expert_kernel(be_ref, x_ref, wg_ref, wu_ref, wd_ref, o_ref):
    del be_ref
    xb = x_ref[...].astype(BF16)
    g = jnp.dot(xb, wg_ref[0].astype(BF16), preferred_element_type=F32)
    u = jnp.dot(xb, wu_ref[0].astype(BF16), preferred_element_type=F32)
    h = (g * jax.nn.sigmoid(g)) * u
    o_ref[...] = jnp.dot(h.astype(BF16), wd_ref[0].astype(BF16), preferred_element_type=F32)


def _expert_blocks(xs, blk_expert, w_g, w_u, w_d):
    n_slots, d = xs.shape
    d_e = w_g.shape[-1]
    n_blk = n_slots // MOE_BLOCK
    return pl.pallas_call(
        _expert_kernel,
        grid_spec=pltpu.PrefetchScalarGridSpec(
            num_scalar_prefetch=1,
            grid=(n_blk,),
            in_specs=[pl.BlockSpec((MOE_BLOCK, d), lambda i, be: (i, 0)),
                      pl.BlockSpec((1, d, d_e), lambda i, be: (be[i], 0, 0)),
                      pl.BlockSpec((1, d, d_e), lambda i, be: (be[i], 0, 0)),
                      pl.BlockSpec((1, d_e, d), lambda i, be: (be[i], 0, 0))],
            out_specs=pl.BlockSpec((MOE_BLOCK, d), lambda i, be: (i, 0))),
        out_shape=jax.ShapeDtypeStruct((n_slots, d), F32),
        compiler_params=_params("arbitrary"),
        name="routed_experts",
    )(blk_expert.astype(jnp.int32), xs, w_g, w_u, w_d)


def _layer_norm(x, g, b):
    mu = jnp.mean(x, axis=-1, keepdims=True)
    var = jnp.mean(jnp.square(x - mu), axis=-1, keepdims=True)
    return ((x - mu) * lax.rsqrt(var + LN_EPS)) * g + b


def _rope(x, positions):
    half = ROPE_DIMS // 2
    inv_freq = jnp.float32(ROPE_THETA) ** (-jnp.arange(half, dtype=F32) / half)
    ang = positions.astype(F32)[..., None] * inv_freq
    shape = ang.shape[:2] + (1,) * (x.ndim - 3) + (half,)
    cos = jnp.cos(ang).reshape(shape)
    sin = jnp.sin(ang).reshape(shape)
    x1, x2, rest = x[..., :half], x[..., half:ROPE_DIMS], x[..., ROPE_DIMS:]
    return jnp.concatenate([x1 * cos - x2 * sin, x1 * sin + x2 * cos, rest], axis=-1)


def _token_mixer(x, positions, w_in, b_f, pe_cmp, w_cmp1, w_cmp2, w_gate, w_br, w_o):
    b_, s_, d_ = x.shape
    xt = x.reshape(b_ * s_, d_)
    proj = _mm(xt, w_in).reshape(b_, s_, IN_WIDTH)
    col = lambda name: proj[..., IN_OFFSETS[name]:IN_OFFSETS[name] + dict(IN_LAYOUT)[name]]
    hd = lambda t: t.reshape(b_, s_, -1, HEAD_DIM)
    heads_first = lambda t: jnp.moveaxis(t, 2, 1)
    as_q = lambda t: (heads_first(t) * ATTN_SCALE).astype(BF16)
    as_kv = lambda t: heads_first(t).astype(BF16)
    merge_heads = lambda o: jnp.moveaxis(o, 1, 2).reshape(b_ * s_, MIX_WIDTH)

    a_q = as_q(_rope(hd(col('a_q')), positions))
    cmp_kv = _compress(jnp.stack([col('a_k_cmp'), col('a_v_cmp')]), pe_cmp, w_cmp1, w_cmp2).astype(BF16)
    o_cmp, picked = _nsa_cmp_select(a_q, cmp_kv[0], cmp_kv[1])
    o_sel = _nsa_selected_attention(a_q, picked, _rope(col('a_k_sel'), positions).astype(BF16),
                                    col('a_v_sel').astype(BF16))
    gates = heads_first(jax.nn.sigmoid(col('a_gate').reshape(b_, s_, HEADS_PER_MIXER, 3)))
    o_a = _nsa_window_merge(a_q, _rope(col('a_k_win'), positions).astype(BF16), col('a_v_win').astype(BF16),
                            o_cmp, o_sel, gates)
    o_b = _stick_breaking_attention(as_q(hd(col('b_q'))), as_kv(hd(col('b_k'))), as_kv(hd(col('b_v'))))
    log_f = jax.nn.log_sigmoid(col('c_f') + b_f)
    cum = jnp.moveaxis(jnp.cumsum(log_f, axis=1), 1, 2)
    o_c = _fox_attention(as_q(hd(col('c_q'))), as_kv(hd(col('c_k'))), as_kv(hd(col('c_v'))), cum)
    d_iq = heads_first(_rope(col('d_iq').reshape(b_, s_, IDX_HEADS, IDX_DIM), positions)).astype(BF16)
    o_d = _dsa_attention(as_q(_rope(hd(col('d_q')), positions)), _rope(col('d_k'), positions).astype(BF16),
                         col('d_v').astype(BF16), d_iq, _rope(col('d_ik'), positions).astype(BF16), col('d_iw'))

    merged = sum(jax.nn.sigmoid(_mm(xt, w_gate[m])) * _mm(merge_heads(o), w_br[m])
                 for m, o in enumerate((o_a, o_b, o_c, o_d)))
    return _mm(merged, w_o).reshape(b_, s_, d_)


def _moe(x, w_router, router_bias, w_exp_gate, w_exp_up, w_exp_down, w_sh_gate, w_sh_up, w_sh_down):
    b_, s_, d_ = x.shape
    n_tok = b_ * s_
    xt = x.reshape(n_tok, d_)
    scores = jax.nn.sigmoid(_mm(xt, w_router))
    biased = scores + router_bias.astype(F32)
    grp = biased.reshape(n_tok, N_GROUPS, N_EXPERTS // N_GROUPS)
    grp_score = jnp.sum(lax.top_k(grp, 2)[0], axis=-1)
    _, top_g = lax.top_k(grp_score, TOPK_GROUPS)
    gmask = jnp.any(top_g[..., :, None] == jnp.arange(N_GROUPS), axis=-2)
    emask = jnp.repeat(gmask, N_EXPERTS // N_GROUPS, axis=-1)
    _, eidx = lax.top_k(jnp.where(emask, biased, NEG_INF), TOP_K)
    w = jnp.take_along_axis(scores, eidx, axis=-1)
    w = w / jnp.sum(w, axis=-1, keepdims=True) * ROUTE_SCALE
    n_assign = n_tok * TOP_K
    flat_e = eidx.reshape(n_assign)
    flat_tok = jnp.repeat(jnp.arange(n_tok, dtype=jnp.int32), TOP_K)
    order = jnp.argsort(flat_e)
    sorted_e = flat_e[order]
    counts = jnp.bincount(flat_e, length=N_EXPERTS)
    padded = (counts + MOE_BLOCK - 1) // MOE_BLOCK * MOE_BLOCK
    pad_end = jnp.cumsum(padded)
    start = jnp.cumsum(counts) - counts
    dest = ((pad_end - padded)[sorted_e] + (jnp.arange(n_assign) - start[sorted_e])).astype(jnp.int32)
    n_slots = n_assign + N_EXPERTS * MOE_BLOCK
    n_blk = n_slots // MOE_BLOCK
    slot_tok = jnp.full((n_slots,), n_tok, jnp.int32).at[dest].set(flat_tok[order])
    blk_expert = jnp.minimum(jnp.searchsorted(pad_end, jnp.arange(n_blk) * MOE_BLOCK, side='right'),
                             N_EXPERTS - 1)
    x_pad = jnp.concatenate([xt, jnp.zeros((1, d_), xt.dtype)], axis=0)
    xs = x_pad[slot_tok]
    y_slots = _expert_blocks(xs, blk_expert, w_exp_gate, w_exp_up, w_exp_down)
    slot_of = jnp.zeros((n_assign,), jnp.int32).at[order].set(dest)
    routed = jnp.einsum('tk,tkd->td', w, y_slots[slot_of].reshape(n_tok, TOP_K, d_))
    sh = _mm(xt, jnp.concatenate([w_sh_gate, w_sh_up], axis=1))
    d_e = w_sh_gate.shape[1]
    shared = _mm(jax.nn.silu(sh[:, :d_e]) * sh[:, d_e:], w_sh_down)
    return (routed + shared).reshape(b_, s_, d_)


def kernel(x, positions, w_in, b_f, pe_cmp, w_cmp1, w_cmp2, w_gate, w_br, w_o, ln1_g, ln1_b, w_router, router_bias, w_exp_gate, w_exp_up, w_exp_down, w_sh_gate, w_sh_up, w_sh_down, ln2_g, ln2_b):
    depth = w_in.shape[0]
    alpha = (2 * depth) ** 0.25
    for l in range(depth):
        mix = _token_mixer(x, positions, w_in[l], b_f[l], pe_cmp[l], w_cmp1[l], w_cmp2[l],
                           w_gate[l], w_br[l], w_o[l])
        x = _layer_norm(alpha * x + mix, ln1_g[l], ln1_b[l])
        ffn = _moe(x, w_router[l], router_bias[l], w_exp_gate[l], w_exp_up[l], w_exp_down[l],
                   w_sh_gate[l], w_sh_up[l], w_sh_down[l])
        x = _layer_norm(alpha * x + ffn, ln2_g[l], ln2_b[l])
    return x
```

```python
import functools
import struct

import jax
import jax.numpy as jnp
import numpy as np
from jax import lax
from jax.experimental import pallas as pl
from jax.experimental.pallas import tpu as pltpu

HEAD_DIM = 64
HEADS_PER_MIXER = 4
MIX_WIDTH = HEADS_PER_MIXER * HEAD_DIM
N_MIXERS = 4
ROPE_THETA = 500000.0
ROPE_DIMS = HEAD_DIM // 4
CMP_LEN = 32
CMP_STRIDE = 16
SEL_BLOCK = 64
SEL_TOPN = 16
WINDOW = 512
FORCE_BONUS = 1.0e6
IDX_HEADS = 4
IDX_DIM = 64
DSA_TOPK_MAX = 256
N_EXPERTS = 64
TOP_K = 8
N_GROUPS = 8
TOPK_GROUPS = 4
ROUTE_SCALE = 2.5
MOE_BLOCK = 512
LN_EPS = 1e-5
NEG_INF = -1e30
TINY = 1e-30

IN_LAYOUT = (
    ('a_q', MIX_WIDTH), ('a_k_cmp', HEAD_DIM), ('a_v_cmp', HEAD_DIM), ('a_k_sel', HEAD_DIM),
    ('a_v_sel', HEAD_DIM), ('a_k_win', HEAD_DIM), ('a_v_win', HEAD_DIM), ('a_gate', 3 * HEADS_PER_MIXER),
    ('b_q', MIX_WIDTH), ('b_k', MIX_WIDTH), ('b_v', MIX_WIDTH),
    ('c_q', MIX_WIDTH), ('c_k', MIX_WIDTH), ('c_v', MIX_WIDTH), ('c_f', HEADS_PER_MIXER),
    ('d_q', MIX_WIDTH), ('d_k', HEAD_DIM), ('d_v', HEAD_DIM),
    ('d_iq', IDX_HEADS * IDX_DIM), ('d_ik', IDX_DIM), ('d_iw', IDX_HEADS),
)
IN_NAMES = tuple(n for n, _ in IN_LAYOUT)
IN_SIZES = tuple(c for _, c in IN_LAYOUT)
IN_WIDTH = sum(IN_SIZES)
IN_OFFSETS = dict(zip(IN_NAMES, np.cumsum((0,) + IN_SIZES[:-1]).tolist()))

LANES = 128
VMEM_LIMIT = 56 * 1024 * 1024
BF16 = jnp.bfloat16
F32 = jnp.float32
ATTN_SCALE = HEAD_DIM ** -0.5


def _round_up(n, m):
    return (n + m - 1) // m * m


def _f32_order_key(f):
    b = struct.unpack('<i', struct.pack('<f', f))[0]
    return b if b >= 0 else b ^ 0x7FFFFFFF


KEY_NEG_INF = _f32_order_key(NEG_INF)
INT_MIN = -2 ** 31


def _dot_nt(a, b):
    return lax.dot_general(a, b, (((1,), (1,)), ((), ())), preferred_element_type=F32)


def _params(*sem):
    return pltpu.CompilerParams(dimension_semantics=sem, vmem_limit_bytes=VMEM_LIMIT)


def _mm_kernel(x_ref, w_ref, o_ref):
    o_ref[...] = jnp.dot(x_ref[...].astype(BF16), w_ref[...], preferred_element_type=F32)


def _mm(x, w, tm=512):
    m, k = x.shape
    n = w.shape[1]
    n_pad = _round_up(n, LANES)
    wb = w.astype(BF16)
    if n_pad != n:
        wb = jnp.pad(wb, ((0, 0), (0, n_pad - n)))
    tm = min(tm, m)
    out = pl.pallas_call(
        _mm_kernel,
        grid=(m // tm,),
        in_specs=[pl.BlockSpec((tm, k), lambda i: (i, 0)),
                  pl.BlockSpec((k, n_pad), lambda i: (0, 0))],
        out_specs=pl.BlockSpec((tm, n_pad), lambda i: (i, 0)),
        out_shape=jax.ShapeDtypeStruct((m, n_pad), F32),
        compiler_params=_params("parallel"),
        name="dense_proj",
    )(x, wb)
    return out[:, :n] if n_pad != n else out


def _softmax_init(m_sc, l_sc, acc_sc):
    m_sc[...] = jnp.full(m_sc.shape, NEG_INF, F32)
    l_sc[...] = jnp.zeros(l_sc.shape, F32)
    acc_sc[...] = jnp.zeros(acc_sc.shape, F32)


def _softmax_step(s, mask, vt, m_sc, l_sc, acc_sc):
    if mask is not None:
        s = jnp.where(mask, s, NEG_INF)
    m_old = m_sc[...]
    m_new = jnp.maximum(m_old, jnp.max(s, axis=1, keepdims=True))
    p = jnp.exp(s - m_new)
    if mask is not None:
        p = jnp.where(mask, p, 0.0)
    alpha = jnp.exp(m_old - m_new)
    l_sc[...] = alpha * l_sc[...] + jnp.sum(p, axis=1, keepdims=True)
    acc_sc[...] = alpha * acc_sc[...] + jnp.dot(p.astype(BF16), vt, preferred_element_type=F32)
    m_sc[...] = m_new


def _tile_positions(q0, k0, tq, tk):
    qpos = q0 + lax.broadcasted_iota(jnp.int32, (tq, tk), 0)
    kpos = k0 + lax.broadcasted_iota(jnp.int32, (tq, tk), 1)
    return qpos, kpos


def _stack_mask(mask, heads):
    tq, tk = mask.shape
    return jnp.broadcast_to(mask[None], (heads, tq, tk)).reshape(heads * tq, tk)


def _fox_kernel(q_ref, k_ref, v_ref, cq_ref, ck_ref, o_ref, m_sc, l_sc, acc_sc, *, tq, tk):
    q0 = pl.program_id(2) * tq
    q = q_ref[0, 0]
    cq = cq_ref[0, 0]
    _softmax_init(m_sc, l_sc, acc_sc)

    def step(j, masked):
        k0 = pl.multiple_of(j * tk, tk)
        kt = k_ref[0, 0, pl.ds(k0, tk), :]
        vt = v_ref[0, 0, pl.ds(k0, tk), :]
        s = _dot_nt(q, kt) + (cq - ck_ref[0, 0, :, pl.ds(k0, tk)])
        mask = None
        if masked:
            qpos, kpos = _tile_positions(q0, k0, tq, tk)
            mask = kpos <= qpos
        _softmax_step(s, mask, vt, m_sc, l_sc, acc_sc)

    n_full = q0 // tk

    def body(j, c):
        step(j, False)
        return c

    lax.fori_loop(0, n_full, body, 0)
    for d in range(tq // tk):
        step(n_full + d, True)
    o_ref[0, 0] = acc_sc[...] / l_sc[...]


def _fox_attention(q, k, v, cum, tq=512, tk=512):
    b, h, s, d = q.shape
    tq, tk = min(tq, s), min(tk, s)
    kern = functools.partial(_fox_kernel, tq=tq, tk=tk)
    return pl.pallas_call(
        kern,
        grid=(b, h, s // tq),
        in_specs=[pl.BlockSpec((1, 1, tq, d), lambda i, j, t: (i, j, t, 0)),
                  pl.BlockSpec((1, 1, s, d), lambda i, j, t: (i, j, 0, 0)),
                  pl.BlockSpec((1, 1, s, d), lambda i, j, t: (i, j, 0, 0)),
                  pl.BlockSpec((1, 1, tq, 1), lambda i, j, t: (i, j, t, 0)),
                  pl.BlockSpec((1, 1, 1, s), lambda i, j, t: (i, j, 0, 0))],
        out_specs=pl.BlockSpec((1, 1, tq, d), lambda i, j, t: (i, j, t, 0)),
        out_shape=jax.ShapeDtypeStruct((b, h, s, d), F32),
        scratch_shapes=[pltpu.VMEM((tq, 1), F32), pltpu.VMEM((tq, 1), F32), pltpu.VMEM((tq, d), F32)],
        compiler_params=_params("parallel", "parallel", "arbitrary"),
        name="forgetting_attention",
    )(q, k, v, cum[..., None], cum[:, :, None, :])


def _sb_kernel(q_ref, k_ref, v_ref, tri_ref, o_ref, r_sc, acc_sc, *, tq, ck):
    q0 = pl.program_id(2) * tq
    q = q_ref[0, 0]
    tri = tri_ref[...]
    r_sc[...] = jnp.zeros(r_sc.shape, F32)
    acc_sc[...] = jnp.zeros(acc_sc.shape, F32)

    def chunk(c, masked):
        k0 = pl.multiple_of(c * ck, ck)
        kt = k_ref[0, 0, pl.ds(k0, ck), :]
        vt = v_ref[0, 0, pl.ds(k0, ck), :]
        z = _dot_nt(q, kt)
        log_fail = -(jnp.maximum(z, 0.0) + jnp.log1p(jnp.exp(-jnp.abs(z))))
        if masked:
            qpos, kpos = _tile_positions(q0, k0, tq, ck)
            mask = kpos < qpos
            log_fail = jnp.where(mask, log_fail, 0.0)
        within = jnp.dot(log_fail.astype(BF16), tri, preferred_element_type=F32)
        a = jnp.exp(z + within + r_sc[...])
        if masked:
            a = jnp.where(mask, a, 0.0)
        acc_sc[...] += jnp.dot(a.astype(BF16), vt, preferred_element_type=F32)
        r_sc[...] += within[:, 0:1]

    n_diag = tq // ck
    n_below = q0 // ck
    for d in reversed(range(n_diag)):
        chunk(n_below + d, True)

    def body(i, c):
        base = n_below - (i + 1) * n_diag
        for d in reversed(range(n_diag)):
            chunk(base + d, False)
        return c

    lax.fori_loop(0, n_below // n_diag, body, 0)
    o_ref[0, 0] = acc_sc[...]


def _stick_breaking_attention(q, k, v, tq=512, ck=128):
    b, h, s, d = q.shape
    tq = min(tq, s)
    tri = jnp.asarray(np.tril(np.ones((ck, ck), np.float32)), dtype=BF16)
    kern = functools.partial(_sb_kernel, tq=tq, ck=ck)
    return pl.pallas_call(
        kern,
        grid=(b, h, s // tq),
        in_specs=[pl.BlockSpec((1, 1, tq, d), lambda i, j, t: (i, j, t, 0)),
                  pl.BlockSpec((1, 1, s, d), lambda i, j, t: (i, j, 0, 0)),
                  pl.BlockSpec((1, 1, s, d), lambda i, j, t: (i, j, 0, 0)),
                  pl.BlockSpec((ck, ck), lambda i, j, t: (0, 0))],
        out_specs=pl.BlockSpec((1, 1, tq, d), lambda i, j, t: (i, j, t, 0)),
        out_shape=jax.ShapeDtypeStruct((b, h, s, d), F32),
        scratch_shapes=[pltpu.VMEM((tq, 1), F32), pltpu.VMEM((tq, d), F32)],
        compiler_params=_params("parallel", "parallel", "arbitrary"),
        name="stick_breaking_attention",
    )(q, k, v, tri)


def _dsa_kernel(iq_ref, iw_ref, ik_ref, q_ref, k_ref, v_ref, triu_ref, o_ref,
                key_sc, m_sc, l_sc, acc_sc, *, tq, tk, topk, heads):
    q0 = pl.program_id(1) * tq
    n_kt = (q0 + tq + tk - 1) // tk
    n_ch = n_kt * (tk // LANES)
    idx_scale = (IDX_DIM ** -0.5) * (IDX_HEADS ** -0.5)
    iw = iw_ref[0]

    def score_tile(j, c):
        k0 = pl.multiple_of(j * tk, tk)
        ikt = ik_ref[0, pl.ds(k0, tk), :]
        score = jnp.zeros((tq, tk), F32)
        for hh in range(IDX_HEADS):
            rel = jnp.maximum(_dot_nt(iq_ref[0, hh], ikt), 0.0)
            score = score + iw[:, hh:hh + 1] * rel
        score = score * idx_scale
        qpos, kpos = _tile_positions(q0, k0, tq, tk)
        score = jnp.where(score == 0.0, 0.0, score)
        score = jnp.where(kpos <= qpos, score, NEG_INF)
        bits = pltpu.bitcast(score, jnp.int32)
        key_sc[:, pl.ds(k0, tk)] = jnp.where(bits >= 0, bits, bits ^ 0x7FFFFFFF)
        return c

    lax.fori_loop(0, n_kt, score_tile, 0)

    def count(pred, n=n_ch):
        def body(c, acc):
            kc = key_sc[:, pl.ds(pl.multiple_of(c * LANES, LANES), LANES)]
            return acc + jnp.where(pred(kc), 1, 0)
        acc = lax.fori_loop(0, n, body, jnp.zeros((tq, LANES), jnp.int32))
        return jnp.sum(acc, axis=1, keepdims=True)

    c0 = count(lambda kc: kc >= 0)
    thr = jnp.where(c0 >= topk, 0, INT_MIN).astype(jnp.int32)

    def bit_step(i, thr):
        cand = thr | lax.shift_left(jnp.int32(1), 30 - i)
        c = count(lambda kc: kc >= cand)
        return jnp.where(c >= topk, cand, thr)

    thr = lax.fori_loop(0, 31, bit_step, thr)
    thr = jnp.maximum(thr, KEY_NEG_INF + 1)
    n_ge = count(lambda kc: kc >= thr)

    n_fix = jnp.where(jnp.max(n_ge) > topk, n_ch, 0)
    n_gt = count(lambda kc: kc > thr, n_fix)
    room = (topk - n_gt).astype(F32)

    def fix(c, seen):
        sl = pl.ds(pl.multiple_of(c * LANES, LANES), LANES)
        kc = key_sc[:, sl]
        eq = kc == thr
        eqf = jnp.where(eq, 1.0, 0.0)
        incl = jnp.dot(eqf.astype(BF16), triu_ref[...], preferred_element_type=F32) + seen
        drop = eq & (incl - eqf >= room)
        key_sc[:, sl] = jnp.where(drop, kc - 1, kc)
        return seen + jnp.sum(eqf, axis=1, keepdims=True)

    lax.fori_loop(0, n_fix, fix, jnp.zeros((tq, 1), F32))

    q = q_ref[0].reshape(heads * tq, HEAD_DIM)
    _softmax_init(m_sc, l_sc, acc_sc)

    def attn_tile(j, c):
        k0 = pl.multiple_of(j * tk, tk)
        kt = k_ref[0, pl.ds(k0, tk), :]
        vt = v_ref[0, pl.ds(k0, tk), :]
        mask = _stack_mask(key_sc[:, pl.ds(k0, tk)] >= thr, heads)
        _softmax_step(_dot_nt(q, kt), mask, vt, m_sc, l_sc, acc_sc)
        return c

    lax.fori_loop(0, n_kt, attn_tile, 0)
    o = acc_sc[...] / jnp.maximum(l_sc[...], TINY)
    o_ref[0] = o.reshape(heads, tq, HEAD_DIM)


def _dsa_attention(q, k, v, iq, ik, iw, tq=128, tk=512):
    b, h, s, d = q.shape
    tq, tk = min(tq, s), min(tk, s)
    topk = min(DSA_TOPK_MAX, s // 4)
    triu = jnp.asarray(np.triu(np.ones((LANES, LANES), np.float32)), dtype=BF16)
    kern = functools.partial(_dsa_kernel, tq=tq, tk=tk, topk=topk, heads=h)
    return pl.pallas_call(
        kern,
        grid=(b, s // tq),
        in_specs=[pl.BlockSpec((1, IDX_HEADS, tq, IDX_DIM), lambda i, t: (i, 0, t, 0)),
                  pl.BlockSpec((1, tq, IDX_HEADS), lambda i, t: (i, t, 0)),
                  pl.BlockSpec((1, s, IDX_DIM), lambda i, t: (i, 0, 0)),
                  pl.BlockSpec((1, h, tq, d), lambda i, t: (i, 0, t, 0)),
                  pl.BlockSpec((1, s, d), lambda i, t: (i, 0, 0)),
                  pl.BlockSpec((1, s, d), lambda i, t: (i, 0, 0)),
                  pl.BlockSpec((LANES, LANES), lambda i, t: (0, 0))],
        out_specs=pl.BlockSpec((1, h, tq, d), lambda i, t: (i, 0, t, 0)),
        out_shape=jax.ShapeDtypeStruct((b, h, s, d), F32),
        scratch_shapes=[pltpu.VMEM((tq, s), jnp.int32), pltpu.VMEM((h * tq, 1), F32),
                        pltpu.VMEM((h * tq, 1), F32), pltpu.VMEM((h * tq, d), F32)],
        compiler_params=_params("parallel", "arbitrary"),
        name="dsa_attention",
    )(iq, iw, ik, q, k, v, triu)


def _compress_kernel(x_ref, pe_ref, w1_ref, w2_ref, o_ref, *, half):
    x = x_ref[0, 0]
    n_rows = x.shape[0]
    ya = jnp.dot((x + pe_ref[0, :, :half]).astype(BF16), w1_ref[0, :half, :], preferred_element_type=F32)
    yb = jnp.dot((x + pe_ref[0, :, half:]).astype(BF16), w1_ref[0, half:, :], preferred_element_type=F32)
    hid = ya + pltpu.roll(yb, n_rows - 1, 0)
    o_ref[0, 0] = jnp.dot(jax.nn.gelu(hid).astype(BF16), w2_ref[0], preferred_element_type=F32)


def _compress(t2, pe_cmp, w_cmp1, w_cmp2):
    _, b, s, d = t2.shape
    assert CMP_LEN == 2 * CMP_STRIDE
    n_rows = s // CMP_STRIDE
    half = CMP_STRIDE * d
    x = t2.reshape(2, b, n_rows, half)
    pe = pe_cmp.reshape(2, 1, CMP_LEN * d)
    kern = functools.partial(_compress_kernel, half=half)
    return pl.pallas_call(
        kern,
        grid=(2, b),
        in_specs=[pl.BlockSpec((1, 1, n_rows, half), lambda j, i: (j, i, 0, 0)),
                  pl.BlockSpec((1, 1, CMP_LEN * d), lambda j, i: (j, 0, 0)),
                  pl.BlockSpec((1, CMP_LEN * d, d), lambda j, i: (j, 0, 0)),
                  pl.BlockSpec((1, d, d), lambda j, i: (j, 0, 0))],
        out_specs=pl.BlockSpec((1, 1, n_rows, d), lambda j, i: (j, i, 0, 0)),
        out_shape=jax.ShapeDtypeStruct((2, b, n_rows, d), F32),
        compiler_params=_params("parallel", "parallel"),
        name="nsa_compress",
    )(x, pe, w_cmp1.astype(BF16), w_cmp2.astype(BF16))


def _nsa_cmp_kernel(q_ref, kc_ref, vc_ref, ov_ref, o_ref, sel_ref, *, tq, heads, n_cmp, top_n):
    q0 = pl.program_id(1) * tq
    q = q_ref[0].reshape(heads * tq, HEAD_DIM)
    kc = kc_ref[0]
    n_rows = kc.shape[0]
    n_sel = sel_ref.shape[-1]
    s = _dot_nt(q, kc)
    qpos, cid = _tile_positions(q0, 0, tq, n_rows)
    mask = _stack_mask((cid * CMP_STRIDE + (CMP_LEN - 1) <= qpos) & (cid < n_cmp), heads)
    s = jnp.where(mask, s, NEG_INF)
    p = jnp.where(mask, jnp.exp(s - jnp.max(s, axis=1, keepdims=True)), 0.0)
    p = p / jnp.maximum(jnp.sum(p, axis=1, keepdims=True), TINY)
    o_ref[0] = jnp.dot(p.astype(BF16), vc_ref[0], preferred_element_type=F32).reshape(heads, tq, HEAD_DIM)

    p_sum = jnp.sum(p.reshape(heads, tq, n_rows), axis=0)
    p_hi = p_sum.astype(BF16)
    p_lo = (p_sum - p_hi.astype(F32)).astype(BF16)
    imp = (jnp.dot(p_hi, ov_ref[...], preferred_element_type=F32)
           + jnp.dot(p_lo, ov_ref[...], preferred_element_type=F32))
    qrow, sid = _tile_positions(q0, 0, tq, n_sel)
    q_blk = qrow // SEL_BLOCK
    forced = (sid == 0) | (sid == q_blk) | (sid == q_blk - 1)
    imp = jnp.where(sid <= q_blk, imp + jnp.where(forced, FORCE_BONUS, 0.0), NEG_INF)

    sidf = sid.astype(F32)
    picked = jnp.zeros((tq, n_sel), F32)
    for _ in range(top_n):
        best = jnp.max(imp, axis=1, keepdims=True)
        first = jnp.min(jnp.where(imp == best, sidf, float(n_sel)), axis=1, keepdims=True)
        hit = sidf == first
        picked = jnp.where(hit, 1.0, picked)
        imp = jnp.where(hit, -3.0e38, imp)
    sel_ref[0] = picked.astype(BF16)


def _nsa_cmp_select(q, k_cmp, v_cmp, tq=128):
    b, h, s, d = q.shape
    tq = min(tq, s)
    n_rows = k_cmp.shape[1]
    n_cmp = (s - CMP_LEN) // CMP_STRIDE + 1
    n_sel = s // SEL_BLOCK
    top_n = min(SEL_TOPN, n_sel)
    c_lo = np.arange(n_rows)[:, None] * CMP_STRIDE
    s_lo = np.arange(n_sel)[None, :] * SEL_BLOCK
    overlap = ((c_lo < s_lo + SEL_BLOCK) & (c_lo + CMP_LEN - 1 >= s_lo) & (np.arange(n_rows)[:, None] < n_cmp))
    overlap = jnp.asarray(overlap.astype(np.float32), dtype=BF16)
    kern = functools.partial(_nsa_cmp_kernel, tq=tq, heads=h, n_cmp=n_cmp, top_n=top_n)
    return pl.pallas_call(
        kern,
        grid=(b, s // tq),
        in_specs=[pl.BlockSpec((1, h, tq, d), lambda i, t: (i, 0, t, 0)),
                  pl.BlockSpec((1, n_rows, d), lambda i, t: (i, 0, 0)),
                  pl.BlockSpec((1, n_rows, d), lambda i, t: (i, 0, 0)),
                  pl.BlockSpec((n_rows, n_sel), lambda i, t: (0, 0))],
        out_specs=[pl.BlockSpec((1, h, tq, d), lambda i, t: (i, 0, t, 0)),
                   pl.BlockSpec((1, tq, n_sel), lambda i, t: (i, t, 0))],
        out_shape=[jax.ShapeDtypeStruct((b, h, s, d), F32), jax.ShapeDtypeStruct((b, s, n_sel), BF16)],
        compiler_params=_params("parallel", "parallel"),
        name="nsa_compressed_select",
    )(q, k_cmp, v_cmp, overlap)


def _nsa_sel_kernel(q_ref, sel_ref, ex_ref, k_ref, v_ref, o_ref, m_sc, l_sc, acc_sc, *, tq, tk, heads):
    q0 = pl.program_id(1) * tq
    n_kt = (q0 + tq + tk - 1) // tk
    q = q_ref[0].reshape(heads * tq, HEAD_DIM)
    picked = sel_ref[0]
    _softmax_init(m_sc, l_sc, acc_sc)

    def tile(j, c):
        k0 = pl.multiple_of(j * tk, tk)
        kt = k_ref[0, pl.ds(k0, tk), :]
        vt = v_ref[0, pl.ds(k0, tk), :]
        in_block = jnp.dot(picked, ex_ref[:, pl.ds(k0, tk)], preferred_element_type=F32)
        qpos, kpos = _tile_positions(q0, k0, tq, tk)
        mask = _stack_mask((in_block > 0.5) & (kpos <= qpos), heads)
        _softmax_step(_dot_nt(q, kt), mask, vt, m_sc, l_sc, acc_sc)
        return c

    lax.fori_loop(0, n_kt, tile, 0)
    o = acc_sc[...] / jnp.maximum(l_sc[...], TINY)
    o_ref[0] = o.reshape(heads, tq, HEAD_DIM)


def _nsa_selected_attention(q, picked, k, v, tq=128, tk=512):
    b, h, s, d = q.shape
    tq, tk = min(tq, s), min(tk, s)
    n_sel = picked.shape[-1]
    expand = (np.arange(n_sel)[:, None] == np.arange(s)[None, :] // SEL_BLOCK).astype(np.float32)
    expand = jnp.asarray(expand, dtype=BF16)
    kern = functools.partial(_nsa_sel_kernel, tq=tq, tk=tk, heads=h)
    return pl.pallas_call(
        kern,
        grid=(b, s // tq),
        in_specs=[pl.BlockSpec((1, h, tq, d), lambda i, t: (i, 0, t, 0)),
                  pl.BlockSpec((1, tq, n_sel), lambda i, t: (i, t, 0)),
                  pl.BlockSpec((n_sel, s), lambda i, t: (0, 0)),
                  pl.BlockSpec((1, s, d), lambda i, t: (i, 0, 0)),
                  pl.BlockSpec((1, s, d), lambda i, t: (i, 0, 0))],
        out_specs=pl.BlockSpec((1, h, tq, d), lambda i, t: (i, 0, t, 0)),
        out_shape=jax.ShapeDtypeStruct((b, h, s, d), F32),
        scratch_shapes=[pltpu.VMEM((h * tq, 1), F32), pltpu.VMEM((h * tq, 1), F32),
                        pltpu.VMEM((h * tq, d), F32)],
        compiler_params=_params("parallel", "arbitrary"),
        name="nsa_selected_attention",
    )(q, picked, expand, k, v)


def _nsa_win_kernel(q_ref, k_ref, v_ref, oc_ref, os_ref, g_ref, o_ref, m_sc, l_sc, acc_sc, *, tq, tk, heads):
    q0 = pl.program_id(1) * tq
    j_lo = jnp.maximum(q0 - WINDOW, 0) // tk
    j_hi = (q0 + tq) // tk
    q = q_ref[0].reshape(heads * tq, HEAD_DIM)
    _softmax_init(m_sc, l_sc, acc_sc)

    def tile(j, c):
        k0 = pl.multiple_of(j * tk, tk)
        kt = k_ref[0, pl.ds(k0, tk), :]
        vt = v_ref[0, pl.ds(k0, tk), :]
        qpos, kpos = _tile_positions(q0, k0, tq, tk)
        mask = _stack_mask((kpos <= qpos) & (kpos > qpos - WINDOW), heads)
        _softmax_step(_dot_nt(q, kt), mask, vt, m_sc, l_sc, acc_sc)
        return c

    lax.fori_loop(j_lo, j_hi, tile, 0)
    o_w = (acc_sc[...] / jnp.maximum(l_sc[...], TINY)).reshape(heads, tq, HEAD_DIM)
    g = g_ref[0]
    o_ref[0] = g[:, :, 0:1] * oc_ref[0] + g[:, :, 1:2] * os_ref[0] + g[:, :, 2:3] * o_w


def _nsa_window_merge(q, k, v, o_c, o_s, gates, tq=256, tk=256):
    b, h, s, d = q.shape
    tq, tk = min(tq, s), min(tk, s)
    assert tq % tk == 0 and WINDOW % tk == 0
    kern = functools.partial(_nsa_win_kernel, tq=tq, tk=tk, heads=h)
    head_spec = pl.BlockSpec((1, h, tq, d), lambda i, t: (i, 0, t, 0))
    return pl.pallas_call(
        kern,
        grid=(b, s // tq),
        in_specs=[head_spec,
                  pl.BlockSpec((1, s, d), lambda i, t: (i, 0, 0)),
                  pl.BlockSpec((1, s, d), lambda i, t: (i, 0, 0)),
                  head_spec, head_spec,
                  pl.BlockSpec((1, h, tq, 3), lambda i, t: (i, 0, t, 0))],
        out_specs=head_spec,
        out_shape=jax.ShapeDtypeStruct((b, h, s, d), F32),
        scratch_shapes=[pltpu.VMEM((h * tq, 1), F32), pltpu.VMEM((h * tq, 1), F32),
                        pltpu.VMEM((h * tq, d), F32)],
        compiler_params=_params("parallel", "parallel"),
        name="nsa_window_merge",
    )(q, k, v, o_c, o_s, gates)


BIG = 3.0e38
ROUTER_FIELDS = 3 * TOP_K


def _first_max(v, lanef):
    best = jnp.max(v, axis=1, keepdims=True)
    first = jnp.min(jnp.where(v == best, lanef, float(LANES)), axis=1, keepdims=True)
    return best, first, lanef == first


def _router_kernel(x_ref, wr_ref, rb_ref, tri_ref, grp_ref, out_ref, cnt_ref, cnt_sc):
    @pl.when(pl.program_id(0) == 0)
    def _():
        cnt_sc[...] = jnp.zeros(cnt_sc.shape, F32)

    tm = x_ref.shape[0]
    per_group = N_EXPERTS // N_GROUPS
    scores = jax.nn.sigmoid(jnp.dot(x_ref[...].astype(BF16), wr_ref[...], preferred_element_type=F32))
    lane = lax.broadcasted_iota(jnp.int32, (tm, LANES), 1)
    lanef = lane.astype(F32)
    biased = jnp.where(lane < N_EXPERTS, scores + rb_ref[...], -BIG)

    grp_of_lane = lane // per_group
    gs = jnp.full((tm, LANES), -BIG, F32)
    for g in range(N_GROUPS):
        v = jnp.where(grp_of_lane == g, biased, -BIG)
        m1, _, hit = _first_max(v, lanef)
        m2 = jnp.max(jnp.where(hit, -BIG, v), axis=1, keepdims=True)
        gs = jnp.where(lane == g, m1 + m2, gs)
    gsel = jnp.zeros((tm, LANES), F32)
    for _ in range(TOPK_GROUPS):
        _, _, hit = _first_max(gs, lanef)
        gsel = jnp.where(hit, 1.0, gsel)
        gs = jnp.where(hit, -BIG, gs)
    emask = jnp.dot(gsel.astype(BF16), grp_ref[...], preferred_element_type=F32) > 0.5
    cand = jnp.where(lane < N_EXPERTS, jnp.where(emask, biased, NEG_INF), -BIG)

    picked = jnp.zeros((tm, LANES), F32)
    hits = []
    for _ in range(TOP_K):
        _, first, hit = _first_max(cand, lanef)
        hits.append((first, hit))
        picked = jnp.where(hit, 1.0, picked)
        cand = jnp.where(hit, -BIG, cand)
    w = scores * picked
    w = w / jnp.sum(w, axis=1, keepdims=True) * ROUTE_SCALE

    rank = jnp.dot(tri_ref[...], picked.astype(BF16), preferred_element_type=F32) + cnt_sc[...]
    cnt_sc[...] += jnp.sum(picked, axis=0, keepdims=True)
    cnt_ref[...] = cnt_sc[...]

    out = jnp.zeros((tm, LANES), F32)
    for k, (first, hit) in enumerate(hits):
        out = jnp.where(lane == k, first, out)
        out = jnp.where(lane == TOP_K + k, jnp.sum(jnp.where(hit, w, 0.0), axis=1, keepdims=True), out)
        out = jnp.where(lane == 2 * TOP_K + k, jnp.sum(jnp.where(hit, rank, 0.0), axis=1, keepdims=True), out)
    out_ref[...] = out


def _router(xt, w_router, router_bias, tm=512):
    n_tok, d = xt.shape
    tm = min(tm, n_tok)
    wr = jnp.pad(w_router.astype(BF16), ((0, 0), (0, LANES - N_EXPERTS)))
    rb = jnp.pad(router_bias.astype(F32), (0, LANES - N_EXPERTS)).reshape(1, LANES)
    tri = jnp.asarray(np.tril(np.ones((tm, tm), np.float32), -1), dtype=BF16)
    per_group = N_EXPERTS // N_GROUPS
    grp = (np.arange(LANES)[:, None] == np.arange(LANES)[None, :] // per_group) & (np.arange(LANES)[None, :] < N_EXPERTS)
    grp = jnp.asarray(grp.astype(np.float32), dtype=BF16)
    out, cnt = pl.pallas_call(
        _router_kernel,
        grid=(n_tok // tm,),
        in_specs=[pl.BlockSpec((tm, d), lambda i: (i, 0)),
                  pl.BlockSpec((d, LANES), lambda i: (0, 0)),
                  pl.BlockSpec((1, LANES), lambda i: (0, 0)),
                  pl.BlockSpec((tm, tm), lambda i: (0, 0)),
                  pl.BlockSpec((LANES, LANES), lambda i: (0, 0))],
        out_specs=[pl.BlockSpec((tm, LANES), lambda i: (i, 0)),
                   pl.BlockSpec((1, LANES), lambda i: (0, 0))],
        out_shape=[jax.ShapeDtypeStruct((n_tok, LANES), F32), jax.ShapeDtypeStruct((1, LANES), F32)],
        scratch_shapes=[pltpu.VMEM((1, LANES), F32)],
        compiler_params=_params("arbitrary"),
        name="moe_router",
    )(xt, wr, rb, tri, grp)
    eidx = out[:, :TOP_K].astype(jnp.int32)
    w = out[:, TOP_K:2 * TOP_K]
    rank = out[:, 2 * TOP_K:3 * TOP_K].astype(jnp.int32)
    return eidx, w, rank, cnt[0, :N_EXPERTS].astype(jnp.int32)


def _expert_kernel(be_ref, x_ref, wg_ref, wu_ref, wd_ref, o_ref):
    del be_ref
    xb = x_ref[...].astype(BF16)
    g = jnp.dot(xb, wg_ref[0].astype(BF16), preferred_element_type=F32)
    u = jnp.dot(xb, wu_ref[0].astype(BF16), preferred_element_type=F32)
    h = (g * jax.nn.sigmoid(g)) * u
    o_ref[...] = jnp.dot(h.astype(BF16), wd_ref[0].astype(BF16), preferred_element_type=F32)


def _expert_blocks(xs, blk_expert, w_g, w_u, w_d):
    n_slots, d = xs.shape
    d_e = w_g.shape[-1]
    n_blk = n_slots // MOE_BLOCK
    return pl.pallas_call(
        _expert_kernel,
        grid_spec=pltpu.PrefetchScalarGridSpec(
            num_scalar_prefetch=1,
            grid=(n_blk,),
            in_specs=[pl.BlockSpec((MOE_BLOCK, d), lambda i, be: (i, 0)),
                      pl.BlockSpec((1, d, d_e), lambda i, be: (be[i], 0, 0)),
                      pl.BlockSpec((1, d, d_e), lambda i, be: (be[i], 0, 0)),
                      pl.BlockSpec((1, d_e, d), lambda i, be: (be[i], 0, 0))],
            out_specs=pl.BlockSpec((MOE_BLOCK, d), lambda i, be: (i, 0))),
        out_shape=jax.ShapeDtypeStruct((n_slots, d), F32),
        compiler_params=_params("arbitrary"),
        name="routed_experts",
    )(blk_expert.astype(jnp.int32), xs, w_g, w_u, w_d)


def _layer_norm(x, g, b):
    mu = jnp.mean(x, axis=-1, keepdims=True)
    var = jnp.mean(jnp.square(x - mu), axis=-1, keepdims=True)
    return ((x - mu) * lax.rsqrt(var + LN_EPS)) * g + b


def _rope(x, positions):
    half = ROPE_DIMS // 2
    inv_freq = jnp.float32(ROPE_THETA) ** (-jnp.arange(half, dtype=F32) / half)
    ang = positions.astype(F32)[..., None] * inv_freq
    shape = ang.shape[:2] + (1,) * (x.ndim - 3) + (half,)
    cos = jnp.cos(ang).reshape(shape)
    sin = jnp.sin(ang).reshape(shape)
    x1, x2, rest = x[..., :half], x[..., half:ROPE_DIMS], x[..., ROPE_DIMS:]
    return jnp.concatenate([x1 * cos - x2 * sin, x1 * sin + x2 * cos, rest], axis=-1)


def _token_mixer(x, positions, w_in, b_f, pe_cmp, w_cmp1, w_cmp2, w_gate, w_br, w_o):
    b_, s_, d_ = x.shape
    xt = x.reshape(b_ * s_, d_)
    proj = _mm(xt, w_in).reshape(b_, s_, IN_WIDTH)
    col = lambda name: proj[..., IN_OFFSETS[name]:IN_OFFSETS[name] + dict(IN_LAYOUT)[name]]
    hd = lambda t: t.reshape(b_, s_, -1, HEAD_DIM)
    heads_first = lambda t: jnp.moveaxis(t, 2, 1)
    as_q = lambda t: (heads_first(t) * ATTN_SCALE).astype(BF16)
    as_kv = lambda t: heads_first(t).astype(BF16)
    merge_heads = lambda o: jnp.moveaxis(o, 1, 2).reshape(b_ * s_, MIX_WIDTH)

    a_q = as_q(_rope(hd(col('a_q')), positions))
    cmp_kv = _compress(jnp.stack([col('a_k_cmp'), col('a_v_cmp')]), pe_cmp, w_cmp1, w_cmp2).astype(BF16)
    o_cmp, picked = _nsa_cmp_select(a_q, cmp_kv[0], cmp_kv[1])
    o_sel = _nsa_selected_attention(a_q, picked, _rope(col('a_k_sel'), positions).astype(BF16),
                                    col('a_v_sel').astype(BF16))
    gates = heads_first(jax.nn.sigmoid(col('a_gate').reshape(b_, s_, HEADS_PER_MIXER, 3)))
    o_a = _nsa_window_merge(a_q, _rope(col('a_k_win'), positions).astype(BF16), col('a_v_win').astype(BF16),
                            o_cmp, o_sel, gates)
    o_b = _stick_breaking_attention(as_q(hd(col('b_q'))), as_kv(hd(col('b_k'))), as_kv(hd(col('b_v'))))
    log_f = jax.nn.log_sigmoid(col('c_f') + b_f)
    cum = jnp.moveaxis(jnp.cumsum(log_f, axis=1), 1, 2)
    o_c = _fox_attention(as_q(hd(col('c_q'))), as_kv(hd(col('c_k'))), as_kv(hd(col('c_v'))), cum)
    d_iq = heads_first(_rope(col('d_iq').reshape(b_, s_, IDX_HEADS, IDX_DIM), positions)).astype(BF16)
    o_d = _dsa_attention(as_q(_rope(hd(col('d_q')), positions)), _rope(col('d_k'), positions).astype(BF16),
                         col('d_v').astype(BF16), d_iq, _rope(col('d_ik'), positions).astype(BF16), col('d_iw'))

    merged = sum(jax.nn.sigmoid(_mm(xt, w_gate[m])) * _mm(merge_heads(o), w_br[m])
                 for m, o in enumerate((o_a, o_b, o_c, o_d)))
    return _mm(merged, w_o).reshape(b_, s_, d_)


def _moe(x, w_router, router_bias, w_exp_gate, w_exp_up, w_exp_down, w_sh_gate, w_sh_up, w_sh_down):
    b_, s_, d_ = x.shape
    n_tok = b_ * s_
    xt = x.reshape(n_tok, d_)
    eidx, w, rank, counts = _router(xt, w_router, router_bias)
    n_assign = n_tok * TOP_K
    padded = (counts + MOE_BLOCK - 1) // MOE_BLOCK * MOE_BLOCK
    pad_end = jnp.cumsum(padded)
    dest = (pad_end - padded)[eidx] + rank
    n_slots = n_assign + N_EXPERTS * MOE_BLOCK
    n_blk = n_slots // MOE_BLOCK
    flat_tok = jnp.repeat(jnp.arange(n_tok, dtype=jnp.int32), TOP_K)
    slot_tok = jnp.full((n_slots,), n_tok, jnp.int32).at[dest.reshape(n_assign)].set(flat_tok)
    blk_expert = jnp.minimum(jnp.searchsorted(pad_end, jnp.arange(n_blk) * MOE_BLOCK, side='right'),
                             N_EXPERTS - 1)
    x_pad = jnp.concatenate([xt, jnp.zeros((1, d_), xt.dtype)], axis=0)
    xs = x_pad[slot_tok]
    y_slots = _expert_blocks(xs, blk_expert, w_exp_gate, w_exp_up, w_exp_down)
    routed = jnp.einsum('tk,tkd->td', w, y_slots[dest.reshape(n_assign)].reshape(n_tok, TOP_K, d_))
    sh = _mm(xt, jnp.concatenate([w_sh_gate, w_sh_up], axis=1))
    d_e = w_sh_gate.shape[1]
    shared = _mm(jax.nn.silu(sh[:, :d_e]) * sh[:, d_e:], w_sh_down)
    return (routed + shared).reshape(b_, s_, d_)


def kernel(x, positions, w_in, b_f, pe_cmp, w_cmp1, w_cmp2, w_gate, w_br, w_o, ln1_g, ln1_b, w_router, router_bias, w_exp_gate, w_exp_up, w_exp_down, w_sh_gate, w_sh_up, w_sh_down, ln2_g, ln2_b):
    depth = w_in.shape[0]
    alpha = (2 * depth) ** 0.25
    for l in range(depth):
        mix = _token_mixer(x, positions, w_in[l], b_f[l], pe_cmp[l], w_cmp1[l], w_cmp2[l],
                           w_gate[l], w_br[l], w_o[l])
        x = _layer_norm(alpha * x + mix, ln1_g[l], ln1_b[l])
        ffn = _moe(x, w_router[l], router_bias[l], w_exp_gate[l], w_exp_up[l], w_exp_down[l],
                   w_sh_gate[l], w_sh_up[l], w_sh_down[l])
        x = _layer_norm(alpha * x + ffn, ln2_g[l], ln2_b[l])
    return x
```

```python
import functools
import struct

import jax
import jax.numpy as jnp
import numpy as np
from jax import lax
from jax.experimental import pallas as pl
from jax.experimental.pallas import tpu as pltpu

HEAD_DIM = 64
HEADS_PER_MIXER = 4
MIX_WIDTH = HEADS_PER_MIXER * HEAD_DIM
N_MIXERS = 4
ROPE_THETA = 500000.0
ROPE_DIMS = HEAD_DIM // 4
CMP_LEN = 32
CMP_STRIDE = 16
SEL_BLOCK = 64
SEL_TOPN = 16
WINDOW = 512
FORCE_BONUS = 1.0e6
IDX_HEADS = 4
IDX_DIM = 64
DSA_TOPK_MAX = 256
N_EXPERTS = 64
TOP_K = 8
N_GROUPS = 8
TOPK_GROUPS = 4
ROUTE_SCALE = 2.5
MOE_BLOCK = 512
LN_EPS = 1e-5
NEG_INF = -1e30
TINY = 1e-30

IN_LAYOUT = (
    ('a_q', MIX_WIDTH), ('a_k_cmp', HEAD_DIM), ('a_v_cmp', HEAD_DIM), ('a_k_sel', HEAD_DIM),
    ('a_v_sel', HEAD_DIM), ('a_k_win', HEAD_DIM), ('a_v_win', HEAD_DIM), ('a_gate', 3 * HEADS_PER_MIXER),
    ('b_q', MIX_WIDTH), ('b_k', MIX_WIDTH), ('b_v', MIX_WIDTH),
    ('c_q', MIX_WIDTH), ('c_k', MIX_WIDTH), ('c_v', MIX_WIDTH), ('c_f', HEADS_PER_MIXER),
    ('d_q', MIX_WIDTH), ('d_k', HEAD_DIM), ('d_v', HEAD_DIM),
    ('d_iq', IDX_HEADS * IDX_DIM), ('d_ik', IDX_DIM), ('d_iw', IDX_HEADS),
)
IN_NAMES = tuple(n for n, _ in IN_LAYOUT)
IN_SIZES = tuple(c for _, c in IN_LAYOUT)
IN_WIDTH = sum(IN_SIZES)
IN_OFFSETS = dict(zip(IN_NAMES, np.cumsum((0,) + IN_SIZES[:-1]).tolist()))

LANES = 128
VMEM_LIMIT = 56 * 1024 * 1024
BF16 = jnp.bfloat16
F32 = jnp.float32
ATTN_SCALE = HEAD_DIM ** -0.5


def _round_up(n, m):
    return (n + m - 1) // m * m


def _f32_order_key(f):
    b = struct.unpack('<i', struct.pack('<f', f))[0]
    return b if b >= 0 else b ^ 0x7FFFFFFF


KEY_NEG_INF = _f32_order_key(NEG_INF)
INT_MIN = -2 ** 31


def _params(*sem):
    return pltpu.CompilerParams(dimension_semantics=sem, vmem_limit_bytes=VMEM_LIMIT)


def _keys_major(k):
    return jnp.swapaxes(k.astype(BF16), -1, -2)


def _mm_kernel(x_ref, w_ref, o_ref):
    o_ref[...] = jnp.dot(x_ref[...].astype(BF16), w_ref[...], preferred_element_type=F32)


def _mm(x, w, tm=512):
    m, k = x.shape
    n = w.shape[1]
    n_pad = _round_up(n, LANES)
    wb = w.astype(BF16)
    if n_pad != n:
        wb = jnp.pad(wb, ((0, 0), (0, n_pad - n)))
    tm = min(tm, m)
    out = pl.pallas_call(
        _mm_kernel,
        grid=(m // tm,),
        in_specs=[pl.BlockSpec((tm, k), lambda i: (i, 0)),
                  pl.BlockSpec((k, n_pad), lambda i: (0, 0))],
        out_specs=pl.BlockSpec((tm, n_pad), lambda i: (i, 0)),
        out_shape=jax.ShapeDtypeStruct((m, n_pad), F32),
        compiler_params=_params("parallel"),
        name="dense_proj",
    )(x, wb)
    return out[:, :n] if n_pad != n else out


def _flash_init(m_sc, acc_sc):
    m_sc[...] = jnp.full(m_sc.shape, NEG_INF, F32)
    acc_sc[...] = jnp.zeros(acc_sc.shape, F32)


def _flash_step(s, v_aug, m_sc, acc_sc):
    m_old = m_sc[...]
    m_new = jnp.maximum(m_old, jnp.max(s, axis=1, keepdims=True))
    p = jnp.concatenate([jnp.exp(s[:, c * LANES:(c + 1) * LANES] - m_new) for c in range(s.shape[1] // LANES)],
                        axis=1)
    acc_sc[...] = jnp.exp(m_old - m_new) * acc_sc[...] + jnp.dot(p.astype(BF16), v_aug,
                                                                  preferred_element_type=F32)
    m_sc[...] = m_new


def _flash_result(acc_sc):
    acc = acc_sc[...]
    return acc[:, :HEAD_DIM] / acc[:, HEAD_DIM:HEAD_DIM + 1]


def _augment_values(v):
    ones = jnp.ones(v.shape[:-1] + (1,), BF16)
    zeros = jnp.zeros(v.shape[:-1] + (LANES - v.shape[-1] - 1,), BF16)
    return jnp.concatenate([v.astype(BF16), ones, zeros], axis=-1)


def _tile_positions(q0, k0, tq, tk):
    qpos = q0 + lax.broadcasted_iota(jnp.int32, (tq, tk), 0)
    kpos = k0 + lax.broadcasted_iota(jnp.int32, (tq, tk), 1)
    return qpos, kpos


def _stack_mask(mask, heads):
    tq, tk = mask.shape
    return jnp.broadcast_to(mask[None], (heads, tq, tk)).reshape(heads * tq, tk)


N_SPLIT = 3


def _split_bf16(t):
    parts, rest = [], t
    for _ in range(N_SPLIT):
        bits = lax.bitcast_convert_type(rest, jnp.uint32) & jnp.uint32(0xFFFF0000)
        part = lax.bitcast_convert_type(bits, F32)
        parts.append(part.astype(BF16))
        rest = rest - part
    return jnp.stack(parts, axis=-1)


def _fox_kernel(q_ref, kt_ref, v_ref, o_ref, m_sc, acc_sc, *, tq, tk):
    q0 = pl.program_id(2) * tq
    q = q_ref[0, 0]
    _flash_init(m_sc, acc_sc)

    def step(j, masked):
        k0 = pl.multiple_of(j * tk, tk)
        s = jnp.dot(q, kt_ref[0, 0, :, pl.ds(k0, tk)], preferred_element_type=F32)
        if masked:
            qpos, kpos = _tile_positions(q0, k0, tq, tk)
            s = jnp.where(kpos <= qpos, s, NEG_INF)
        _flash_step(s, v_ref[0, 0, pl.ds(k0, tk), :], m_sc, acc_sc)

    n_full = q0 // tk

    def body(j, c):
        step(j, False)
        return c

    lax.fori_loop(0, n_full, body, 0)
    for d in range(tq // tk):
        step(n_full + d, True)
    o_ref[0, 0] = _flash_result(acc_sc)


def _fox_attention(q, k, v, cum, tq=512, tk=512):
    b, h, s, d = q.shape
    tq, tk = min(tq, s), min(tk, s)
    cum_terms = _split_bf16(cum)
    ones = jnp.ones_like(cum_terms)
    pad = jnp.zeros((b, h, s, LANES - d - 2 * N_SPLIT), BF16)
    q_aug = jnp.concatenate([q, cum_terms, ones, pad], axis=-1)
    k_aug = jnp.concatenate([k, ones, -cum_terms, pad], axis=-1)
    kern = functools.partial(_fox_kernel, tq=tq, tk=tk)
    return pl.pallas_call(
        kern,
        grid=(b, h, s // tq),
        in_specs=[pl.BlockSpec((1, 1, tq, LANES), lambda i, j, t: (i, j, t, 0)),
                  pl.BlockSpec((1, 1, LANES, s), lambda i, j, t: (i, j, 0, 0)),
                  pl.BlockSpec((1, 1, s, LANES), lambda i, j, t: (i, j, 0, 0))],
        out_specs=pl.BlockSpec((1, 1, tq, d), lambda i, j, t: (i, j, t, 0)),
        out_shape=jax.ShapeDtypeStruct((b, h, s, d), F32),
        scratch_shapes=[pltpu.VMEM((tq, LANES), F32), pltpu.VMEM((tq, LANES), F32)],
        compiler_params=_params("parallel", "parallel", "arbitrary"),
        name="forgetting_attention",
    )(q_aug, _keys_major(k_aug), _augment_values(v))


def _sb_kernel(q_ref, kt_ref, v_ref, tri_ref, o_ref, r_sc, acc_sc, *, tq, ck):
    q0 = pl.program_id(2) * tq
    q = q_ref[0, 0]
    tri = tri_ref[...]
    r_sc[...] = jnp.zeros(r_sc.shape, F32)
    acc_sc[...] = jnp.zeros(acc_sc.shape, F32)

    def chunk(c, masked):
        k0 = pl.multiple_of(c * ck, ck)
        vt = v_ref[0, 0, pl.ds(k0, ck), :]
        z = jnp.dot(q, kt_ref[0, 0, :, pl.ds(k0, ck)], preferred_element_type=F32)
        neg_z = -z
        log_fail = jnp.minimum(neg_z, 0.0) - jnp.log(1.0 + jnp.exp(jnp.minimum(z, neg_z)))
        if masked:
            qpos, kpos = _tile_positions(q0, k0, tq, ck)
            mask = kpos < qpos
            log_fail = jnp.where(mask, log_fail, 0.0)
        within = jnp.dot(log_fail.astype(BF16), tri, preferred_element_type=F32)
        r = r_sc[...]
        t = z + within
        a = jnp.concatenate([jnp.exp(t[:, c * LANES:(c + 1) * LANES] + r) for c in range(ck // LANES)], axis=1)
        if masked:
            a = jnp.where(mask, a, 0.0)
        acc_sc[...] += jnp.dot(a.astype(BF16), vt, preferred_element_type=F32)
        r_sc[...] = r + within[:, 0:1]

    n_diag = tq // ck
    n_below = q0 // ck
    for d in reversed(range(n_diag)):
        chunk(n_below + d, True)

    def body(i, c):
        base = n_below - (i + 1) * n_diag
        for d in reversed(range(n_diag)):
            chunk(base + d, False)
        return c

    lax.fori_loop(0, n_below // n_diag, body, 0)
    o_ref[0, 0] = acc_sc[...]


def _stick_breaking_attention(q, k, v, tq=512, ck=256):
    b, h, s, d = q.shape
    tq = min(tq, s)
    tri = jnp.asarray(np.tril(np.ones((ck, ck), np.float32)), dtype=BF16)
    kern = functools.partial(_sb_kernel, tq=tq, ck=ck)
    return pl.pallas_call(
        kern,
        grid=(b, h, s // tq),
        in_specs=[pl.BlockSpec((1, 1, tq, d), lambda i, j, t: (i, j, t, 0)),
                  pl.BlockSpec((1, 1, d, s), lambda i, j, t: (i, j, 0, 0)),
                  pl.BlockSpec((1, 1, s, d), lambda i, j, t: (i, j, 0, 0)),
                  pl.BlockSpec((ck, ck), lambda i, j, t: (0, 0))],
        out_specs=pl.BlockSpec((1, 1, tq, d), lambda i, j, t: (i, j, t, 0)),
        out_shape=jax.ShapeDtypeStruct((b, h, s, d), F32),
        scratch_shapes=[pltpu.VMEM((tq, LANES), F32), pltpu.VMEM((tq, d), F32)],
        compiler_params=_params("parallel", "parallel", "arbitrary"),
        name="stick_breaking_attention",
    )(q, _keys_major(k), v, tri)


def _dsa_kernel(iq_ref, iw_ref, ikt_ref, q_ref, kt_ref, v_ref, triu_ref, o_ref,
                key_sc, m_sc, acc_sc, *, tq, tk, topk, heads):
    q0 = pl.program_id(1) * tq
    n_kt = (q0 + tq + tk - 1) // tk
    per_tile = tk // LANES
    idx_scale = (IDX_DIM ** -0.5) * (IDX_HEADS ** -0.5)
    iw = iw_ref[0]
    iq = iq_ref[0].reshape(IDX_HEADS * tq, IDX_DIM)

    def score_tile(j, c):
        k0 = pl.multiple_of(j * tk, tk)
        rel = jnp.maximum(jnp.dot(iq, ikt_ref[0, :, pl.ds(k0, tk)], preferred_element_type=F32), 0.0)
        rel = rel.reshape(IDX_HEADS, tq, tk)
        score = iw[:, 0:1] * rel[0]
        for hh in range(1, IDX_HEADS):
            score = score + iw[:, hh:hh + 1] * rel[hh]
        score = score * idx_scale
        qpos, kpos = _tile_positions(q0, k0, tq, tk)
        score = jnp.where(score == 0.0, 0.0, score)
        score = jnp.where(kpos <= qpos, score, NEG_INF)
        bits = pltpu.bitcast(score, jnp.int32)
        key_sc[:, pl.ds(k0, tk)] = jnp.where(bits >= 0, bits, bits ^ 0x7FFFFFFF)
        return c

    lax.fori_loop(0, n_kt, score_tile, 0)

    def count(pred, n=n_kt):
        def body(j, acc):
            k0 = pl.multiple_of(j * tk, tk)
            for u in range(per_tile):
                acc = acc + jnp.where(pred(key_sc[:, pl.ds(k0 + u * LANES, LANES)]), 1, 0)
            return acc
        acc = lax.fori_loop(0, n, body, jnp.zeros((tq, LANES), jnp.int32))
        return jnp.sum(acc, axis=1, keepdims=True)

    c0 = count(lambda kc: kc >= 0)
    thr = jnp.where(c0 >= topk, 0, INT_MIN).astype(jnp.int32)
    cnt = jnp.where(c0 >= topk, c0, n_kt * tk)

    def unsettled(state):
        i, _, cnt = state
        return (i < 31) & (jnp.max(jnp.abs(cnt - topk)) > 0)

    def bit_step(state):
        i, thr, cnt = state
        cand = thr | lax.shift_left(jnp.int32(1), 30 - i)
        c = count(lambda kc: kc >= cand)
        ok = c >= topk
        return i + 1, jnp.where(ok, cand, thr), jnp.where(ok, c, cnt)

    _, thr, cnt = lax.while_loop(unsettled, bit_step, (jnp.int32(0), thr, cnt))
    tied = (thr > KEY_NEG_INF) & (cnt > topk)
    thr = jnp.maximum(thr, KEY_NEG_INF + 1)

    n_fix = jnp.where(jnp.max(jnp.where(tied, 1, 0)) > 0, n_kt, 0)
    n_gt = count(lambda kc: kc > thr, n_fix)
    room = (topk - n_gt).astype(F32)

    def fix(c, seen):
        sl = pl.ds(pl.multiple_of(c * LANES, LANES), LANES)
        kc = key_sc[:, sl]
        eq = kc == thr
        eqf = jnp.where(eq, 1.0, 0.0)
        incl = jnp.dot(eqf.astype(BF16), triu_ref[...], preferred_element_type=F32) + seen
        drop = eq & (incl - eqf >= room)
        key_sc[:, sl] = jnp.where(drop, kc - 1, kc)
        return seen + jnp.sum(eqf, axis=1, keepdims=True)

    lax.fori_loop(0, n_fix * per_tile, fix, jnp.zeros((tq, 1), F32))

    q = q_ref[0].reshape(heads * tq, HEAD_DIM)
    _flash_init(m_sc, acc_sc)

    def attn_tile(j, c):
        k0 = pl.multiple_of(j * tk, tk)
        mask = _stack_mask(key_sc[:, pl.ds(k0, tk)] >= thr, heads)
        s = jnp.dot(q, kt_ref[0, :, pl.ds(k0, tk)], preferred_element_type=F32)
        _flash_step(jnp.where(mask, s, NEG_INF), v_ref[0, pl.ds(k0, tk), :], m_sc, acc_sc)
        return c

    lax.fori_loop(0, n_kt, attn_tile, 0)
    o_ref[0] = _flash_result(acc_sc).reshape(heads, tq, HEAD_DIM)


def _dsa_attention(q, k, v, iq, ik, iw, tq=128, tk=512):
    b, h, s, d = q.shape
    tq, tk = min(tq, s), min(tk, s)
    topk = min(DSA_TOPK_MAX, s // 4)
    triu = jnp.asarray(np.triu(np.ones((LANES, LANES), np.float32)), dtype=BF16)
    kern = functools.partial(_dsa_kernel, tq=tq, tk=tk, topk=topk, heads=h)
    return pl.pallas_call(
        kern,
        grid=(b, s // tq),
        in_specs=[pl.BlockSpec((1, IDX_HEADS, tq, IDX_DIM), lambda i, t: (i, 0, t, 0)),
                  pl.BlockSpec((1, tq, IDX_HEADS), lambda i, t: (i, t, 0)),
                  pl.BlockSpec((1, IDX_DIM, s), lambda i, t: (i, 0, 0)),
                  pl.BlockSpec((1, h, tq, d), lambda i, t: (i, 0, t, 0)),
                  pl.BlockSpec((1, d, s), lambda i, t: (i, 0, 0)),
                  pl.BlockSpec((1, s, LANES), lambda i, t: (i, 0, 0)),
                  pl.BlockSpec((LANES, LANES), lambda i, t: (0, 0))],
        out_specs=pl.BlockSpec((1, h, tq, d), lambda i, t: (i, 0, t, 0)),
        out_shape=jax.ShapeDtypeStruct((b, h, s, d), F32),
        scratch_shapes=[pltpu.VMEM((tq, s), jnp.int32), pltpu.VMEM((h * tq, LANES), F32),
                        pltpu.VMEM((h * tq, LANES), F32)],
        compiler_params=_params("parallel", "arbitrary"),
        name="dsa_attention",
    )(iq, iw, _keys_major(ik), q, _keys_major(k), _augment_values(v), triu)


def _compress_kernel(x_ref, pe_ref, w1_ref, w2_ref, o_ref, *, half):
    x = x_ref[0, 0]
    n_rows = x.shape[0]
    ya = jnp.dot((x + pe_ref[0, :, :half]).astype(BF16), w1_ref[0, :half, :], preferred_element_type=F32)
    yb = jnp.dot((x + pe_ref[0, :, half:]).astype(BF16), w1_ref[0, half:, :], preferred_element_type=F32)
    hid = ya + pltpu.roll(yb, n_rows - 1, 0)
    o_ref[0, 0] = jnp.dot(jax.nn.gelu(hid).astype(BF16), w2_ref[0], preferred_element_type=F32)


def _compress(t2, pe_cmp, w_cmp1, w_cmp2):
    _, b, s, d = t2.shape
    assert CMP_LEN == 2 * CMP_STRIDE
    n_rows = s // CMP_STRIDE
    half = CMP_STRIDE * d
    x = t2.reshape(2, b, n_rows, half)
    pe = pe_cmp.reshape(2, 1, CMP_LEN * d)
    kern = functools.partial(_compress_kernel, half=half)
    return pl.pallas_call(
        kern,
        grid=(2, b),
        in_specs=[pl.BlockSpec((1, 1, n_rows, half), lambda j, i: (j, i, 0, 0)),
                  pl.BlockSpec((1, 1, CMP_LEN * d), lambda j, i: (j, 0, 0)),
                  pl.BlockSpec((1, CMP_LEN * d, d), lambda j, i: (j, 0, 0)),
                  pl.BlockSpec((1, d, d), lambda j, i: (j, 0, 0))],
        out_specs=pl.BlockSpec((1, 1, n_rows, d), lambda j, i: (j, i, 0, 0)),
        out_shape=jax.ShapeDtypeStruct((2, b, n_rows, d), F32),
        compiler_params=_params("parallel", "parallel"),
        name="nsa_compress",
    )(x, pe, w_cmp1.astype(BF16), w_cmp2.astype(BF16))


def _nsa_cmp_kernel(q_ref, kct_ref, vc_ref, ov_ref, o_ref, sel_ref, *, tq, heads, n_cmp, top_n):
    q0 = pl.program_id(1) * tq
    q = q_ref[0].reshape(heads * tq, HEAD_DIM)
    n_rows = kct_ref.shape[-1]
    n_sel = sel_ref.shape[-1]
    s = jnp.dot(q, kct_ref[0], preferred_element_type=F32)
    qpos, cid = _tile_positions(q0, 0, tq, n_rows)
    mask = _stack_mask((cid * CMP_STRIDE + (CMP_LEN - 1) <= qpos) & (cid < n_cmp), heads)
    s = jnp.where(mask, s, NEG_INF)
    p = jnp.where(mask, jnp.exp(s - jnp.max(s, axis=1, keepdims=True)), 0.0)
    p = p / jnp.maximum(jnp.sum(p, axis=1, keepdims=True), TINY)
    o_ref[0] = jnp.dot(p.astype(BF16), vc_ref[0], preferred_element_type=F32).reshape(heads, tq, HEAD_DIM)

    p_sum = jnp.sum(p.reshape(heads, tq, n_rows), axis=0)
    p_hi = p_sum.astype(BF16)
    p_lo = (p_sum - p_hi.astype(F32)).astype(BF16)
    imp = (jnp.dot(p_hi, ov_ref[...], preferred_element_type=F32)
           + jnp.dot(p_lo, ov_ref[...], preferred_element_type=F32))
    qrow, sid = _tile_positions(q0, 0, tq, n_sel)
    q_blk = qrow // SEL_BLOCK
    forced = (sid == 0) | (sid == q_blk) | (sid == q_blk - 1)
    imp = jnp.where(sid <= q_blk, imp + jnp.where(forced, FORCE_BONUS, 0.0), NEG_INF)

    sidf = sid.astype(F32)
    picked = jnp.zeros((tq, n_sel), F32)
    for _ in range(top_n):
        best = jnp.max(imp, axis=1, keepdims=True)
        first = jnp.min(jnp.where(imp == best, sidf, float(n_sel)), axis=1, keepdims=True)
        hit = sidf == first
        picked = jnp.where(hit, 1.0, picked)
        imp = jnp.where(hit, -3.0e38, imp)
    sel_ref[0] = picked.astype(BF16)


def _nsa_cmp_select(q, k_cmp, v_cmp, tq=128):
    b, h, s, d = q.shape
    tq = min(tq, s)
    n_rows = k_cmp.shape[1]
    n_cmp = (s - CMP_LEN) // CMP_STRIDE + 1
    n_sel = s // SEL_BLOCK
    top_n = min(SEL_TOPN, n_sel)
    c_lo = np.arange(n_rows)[:, None] * CMP_STRIDE
    s_lo = np.arange(n_sel)[None, :] * SEL_BLOCK
    overlap = ((c_lo < s_lo + SEL_BLOCK) & (c_lo + CMP_LEN - 1 >= s_lo) & (np.arange(n_rows)[:, None] < n_cmp))
    overlap = jnp.asarray(overlap.astype(np.float32), dtype=BF16)
    kern = functools.partial(_nsa_cmp_kernel, tq=tq, heads=h, n_cmp=n_cmp, top_n=top_n)
    return pl.pallas_call(
        kern,
        grid=(b, s // tq),
        in_specs=[pl.BlockSpec((1, h, tq, d), lambda i, t: (i, 0, t, 0)),
                  pl.BlockSpec((1, d, n_rows), lambda i, t: (i, 0, 0)),
                  pl.BlockSpec((1, n_rows, d), lambda i, t: (i, 0, 0)),
                  pl.BlockSpec((n_rows, n_sel), lambda i, t: (0, 0))],
        out_specs=[pl.BlockSpec((1, h, tq, d), lambda i, t: (i, 0, t, 0)),
                   pl.BlockSpec((1, tq, n_sel), lambda i, t: (i, t, 0))],
        out_shape=[jax.ShapeDtypeStruct((b, h, s, d), F32), jax.ShapeDtypeStruct((b, s, n_sel), BF16)],
        compiler_params=_params("parallel", "parallel"),
        name="nsa_compressed_select",
    )(q, _keys_major(k_cmp), v_cmp, overlap)


def _nsa_sel_kernel(q_ref, sel_ref, ex_ref, kt_ref, v_ref, o_ref, m_sc, acc_sc, *, tq, tk, heads):
    q0 = pl.program_id(1) * tq
    n_kt = (q0 + tq + tk - 1) // tk
    q = q_ref[0].reshape(heads * tq, HEAD_DIM)
    picked = sel_ref[0]
    _flash_init(m_sc, acc_sc)

    def tile(j, c):
        k0 = pl.multiple_of(j * tk, tk)
        in_block = jnp.dot(picked, ex_ref[:, pl.ds(k0, tk)], preferred_element_type=F32)
        qpos, kpos = _tile_positions(q0, k0, tq, tk)
        mask = _stack_mask((in_block > 0.5) & (kpos <= qpos), heads)
        s = jnp.dot(q, kt_ref[0, :, pl.ds(k0, tk)], preferred_element_type=F32)
        _flash_step(jnp.where(mask, s, NEG_INF), v_ref[0, pl.ds(k0, tk), :], m_sc, acc_sc)
        return c

    lax.fori_loop(0, n_kt, tile, 0)
    o_ref[0] = _flash_result(acc_sc).reshape(heads, tq, HEAD_DIM)


def _nsa_selected_attention(q, picked, k, v, tq=128, tk=512):
    b, h, s, d = q.shape
    tq, tk = min(tq, s), min(tk, s)
    n_sel = picked.shape[-1]
    expand = (np.arange(n_sel)[:, None] == np.arange(s)[None, :] // SEL_BLOCK).astype(np.float32)
    expand = jnp.asarray(expand, dtype=BF16)
    kern = functools.partial(_nsa_sel_kernel, tq=tq, tk=tk, heads=h)
    return pl.pallas_call(
        kern,
        grid=(b, s // tq),
        in_specs=[pl.BlockSpec((1, h, tq, d), lambda i, t: (i, 0, t, 0)),
                  pl.BlockSpec((1, tq, n_sel), lambda i, t: (i, t, 0)),
                  pl.BlockSpec((n_sel, s), lambda i, t: (0, 0)),
                  pl.BlockSpec((1, d, s), lambda i, t: (i, 0, 0)),
                  pl.BlockSpec((1, s, LANES), lambda i, t: (i, 0, 0))],
        out_specs=pl.BlockSpec((1, h, tq, d), lambda i, t: (i, 0, t, 0)),
        out_shape=jax.ShapeDtypeStruct((b, h, s, d), F32),
        scratch_shapes=[pltpu.VMEM((h * tq, LANES), F32), pltpu.VMEM((h * tq, LANES), F32)],
        compiler_params=_params("parallel", "arbitrary"),
        name="nsa_selected_attention",
    )(q, picked, expand, _keys_major(k), _augment_values(v))


def _nsa_win_kernel(q_ref, kt_ref, v_ref, oc_ref, os_ref, g_ref, o_ref, m_sc, acc_sc, *, tq, tk, heads):
    q0 = pl.program_id(1) * tq
    j_lo = jnp.maximum(q0 - WINDOW, 0) // tk
    j_hi = (q0 + tq) // tk
    q = q_ref[0].reshape(heads * tq, HEAD_DIM)
    _flash_init(m_sc, acc_sc)

    def tile(j, c):
        k0 = pl.multiple_of(j * tk, tk)
        qpos, kpos = _tile_positions(q0, k0, tq, tk)
        mask = _stack_mask((kpos <= qpos) & (kpos > qpos - WINDOW), heads)
        s = jnp.dot(q, kt_ref[0, :, pl.ds(k0, tk)], preferred_element_type=F32)
        _flash_step(jnp.where(mask, s, NEG_INF), v_ref[0, pl.ds(k0, tk), :], m_sc, acc_sc)
        return c

    lax.fori_loop(j_lo, j_hi, tile, 0)
    o_w = _flash_result(acc_sc).reshape(heads, tq, HEAD_DIM)
    g = g_ref[0]
    o_ref[0] = g[:, :, 0:1] * oc_ref[0] + g[:, :, 1:2] * os_ref[0] + g[:, :, 2:3] * o_w


def _nsa_window_merge(q, k, v, o_c, o_s, gates, tq=256, tk=256):
    b, h, s, d = q.shape
    tq, tk = min(tq, s), min(tk, s)
    assert tq % tk == 0 and WINDOW % tk == 0
    kern = functools.partial(_nsa_win_kernel, tq=tq, tk=tk, heads=h)
    head_spec = pl.BlockSpec((1, h, tq, d), lambda i, t: (i, 0, t, 0))
    return pl.pallas_call(
        kern,
        grid=(b, s // tq),
        in_specs=[head_spec,
                  pl.BlockSpec((1, d, s), lambda i, t: (i, 0, 0)),
                  pl.BlockSpec((1, s, LANES), lambda i, t: (i, 0, 0)),
                  head_spec, head_spec,
                  pl.BlockSpec((1, h, tq, 3), lambda i, t: (i, 0, t, 0))],
        out_specs=head_spec,
        out_shape=jax.ShapeDtypeStruct((b, h, s, d), F32),
        scratch_shapes=[pltpu.VMEM((h * tq, LANES), F32), pltpu.VMEM((h * tq, LANES), F32)],
        compiler_params=_params("parallel", "parallel"),
        name="nsa_window_merge",
    )(q, _keys_major(k), _augment_values(v), o_c, o_s, gates)


BIG = 3.0e38


def _first_max(v, lanef):
    best = jnp.max(v, axis=1, keepdims=True)
    first = jnp.min(jnp.where(v == best, lanef, float(LANES)), axis=1, keepdims=True)
    return best, first, lanef == first


def _router_kernel(x_ref, wr_ref, rb_ref, tri_ref, grp_ref, out_ref, cnt_ref, cnt_sc):
    @pl.when(pl.program_id(0) == 0)
    def _():
        cnt_sc[...] = jnp.zeros(cnt_sc.shape, F32)

    tm = x_ref.shape[0]
    per_group = N_EXPERTS // N_GROUPS
    scores = jax.nn.sigmoid(jnp.dot(x_ref[...].astype(BF16), wr_ref[...], preferred_element_type=F32))
    lane = lax.broadcasted_iota(jnp.int32, (tm, LANES), 1)
    lanef = lane.astype(F32)
    biased = jnp.where(lane < N_EXPERTS, scores + rb_ref[...], -BIG)

    grp_of_lane = lane // per_group
    gs = jnp.full((tm, LANES), -BIG, F32)
    for g in range(N_GROUPS):
        v = jnp.where(grp_of_lane == g, biased, -BIG)
        m1, _, hit = _first_max(v, lanef)
        m2 = jnp.max(jnp.where(hit, -BIG, v), axis=1, keepdims=True)
        gs = jnp.where(lane == g, m1 + m2, gs)
    gsel = jnp.zeros((tm, LANES), F32)
    for _ in range(TOPK_GROUPS):
        _, _, hit = _first_max(gs, lanef)
        gsel = jnp.where(hit, 1.0, gsel)
        gs = jnp.where(hit, -BIG, gs)
    emask = jnp.dot(gsel.astype(BF16), grp_ref[...], preferred_element_type=F32) > 0.5
    cand = jnp.where(lane < N_EXPERTS, jnp.where(emask, biased, NEG_INF), -BIG)

    picked = jnp.zeros((tm, LANES), F32)
    hits = []
    for _ in range(TOP_K):
        _, first, hit = _first_max(cand, lanef)
        hits.append((first, hit))
        picked = jnp.where(hit, 1.0, picked)
        cand = jnp.where(hit, -BIG, cand)
    w = scores * picked
    w = w / jnp.sum(w, axis=1, keepdims=True) * ROUTE_SCALE

    rank = jnp.dot(tri_ref[...], picked.astype(BF16), preferred_element_type=F32) + cnt_sc[...]
    cnt_sc[...] += jnp.sum(picked, axis=0, keepdims=True)
    cnt_ref[...] = cnt_sc[...]

    out = jnp.zeros((tm, LANES), F32)
    for k, (first, hit) in enumerate(hits):
        out = jnp.where(lane == k, first, out)
        out = jnp.where(lane == TOP_K + k, jnp.sum(jnp.where(hit, w, 0.0), axis=1, keepdims=True), out)
        out = jnp.where(lane == 2 * TOP_K + k, jnp.sum(jnp.where(hit, rank, 0.0), axis=1, keepdims=True), out)
    out_ref[...] = out


def _router(xt, w_router, router_bias, tm=512):
    n_tok, d = xt.shape
    tm = min(tm, n_tok)
    wr = jnp.pad(w_router.astype(BF16), ((0, 0), (0, LANES - N_EXPERTS)))
    rb = jnp.pad(router_bias.astype(F32), (0, LANES - N_EXPERTS)).reshape(1, LANES)
    tri = jnp.asarray(np.tril(np.ones((tm, tm), np.float32), -1), dtype=BF16)
    per_group = N_EXPERTS // N_GROUPS
    grp = (np.arange(LANES)[:, None] == np.arange(LANES)[None, :] // per_group) & (np.arange(LANES)[None, :] < N_EXPERTS)
    grp = jnp.asarray(grp.astype(np.float32), dtype=BF16)
    out, cnt = pl.pallas_call(
        _router_kernel,
        grid=(n_tok // tm,),
        in_specs=[pl.BlockSpec((tm, d), lambda i: (i, 0)),
                  pl.BlockSpec((d, LANES), lambda i: (0, 0)),
                  pl.BlockSpec((1, LANES), lambda i: (0, 0)),
                  pl.BlockSpec((tm, tm), lambda i: (0, 0)),
                  pl.BlockSpec((LANES, LANES), lambda i: (0, 0))],
        out_specs=[pl.BlockSpec((tm, LANES), lambda i: (i, 0)),
                   pl.BlockSpec((1, LANES), lambda i: (0, 0))],
        out_shape=[jax.ShapeDtypeStruct((n_tok, LANES), F32), jax.ShapeDtypeStruct((1, LANES), F32)],
        scratch_shapes=[pltpu.VMEM((1, LANES), F32)],
        compiler_params=_params("arbitrary"),
        name="moe_router",
    )(xt, wr, rb, tri, grp)
    eidx = out[:, :TOP_K].astype(jnp.int32)
    w = out[:, TOP_K:2 * TOP_K]
    rank = out[:, 2 * TOP_K:3 * TOP_K].astype(jnp.int32)
    return eidx, w, rank, cnt[0, :N_EXPERTS].astype(jnp.int32)


def _expert_kernel(be_ref, x_ref, wg_ref, wu_ref, wd_ref, o_ref):
    del be_ref
    xb = x_ref[...].astype(BF16)
    g = jnp.dot(xb, wg_ref[0].astype(BF16), preferred_element_type=F32)
    u = jnp.dot(xb, wu_ref[0].astype(BF16), preferred_element_type=F32)
    h = (g * jax.nn.sigmoid(g)) * u
    o_ref[...] = jnp.dot(h.astype(BF16), wd_ref[0].astype(BF16), preferred_element_type=F32)


def _expert_blocks(xs, blk_expert, w_g, w_u, w_d):
    n_slots, d = xs.shape
    d_e = w_g.shape[-1]
    n_blk = n_slots // MOE_BLOCK
    return pl.pallas_call(
        _expert_kernel,
        grid_spec=pltpu.PrefetchScalarGridSpec(
            num_scalar_prefetch=1,
            grid=(n_blk,),
            in_specs=[pl.BlockSpec((MOE_BLOCK, d), lambda i, be: (i, 0)),
                      pl.BlockSpec((1, d, d_e), lambda i, be: (be[i], 0, 0)),
                      pl.BlockSpec((1, d, d_e), lambda i, be: (be[i], 0, 0)),
                      pl.BlockSpec((1, d_e, d), lambda i, be: (be[i], 0, 0))],
            out_specs=pl.BlockSpec((MOE_BLOCK, d), lambda i, be: (i, 0))),
        out_shape=jax.ShapeDtypeStruct((n_slots, d), F32),
        compiler_params=_params("arbitrary"),
        name="routed_experts",
    )(blk_expert.astype(jnp.int32), xs, w_g, w_u, w_d)


def _layer_norm(x, g, b):
    mu = jnp.mean(x, axis=-1, keepdims=True)
    var = jnp.mean(jnp.square(x - mu), axis=-1, keepdims=True)
    return ((x - mu) * lax.rsqrt(var + LN_EPS)) * g + b


def _rope(x, positions):
    half = ROPE_DIMS // 2
    inv_freq = jnp.float32(ROPE_THETA) ** (-jnp.arange(half, dtype=F32) / half)
    ang = positions.astype(F32)[..., None] * inv_freq
    shape = ang.shape[:2] + (1,) * (x.ndim - 3) + (half,)
    cos = jnp.cos(ang).reshape(shape)
    sin = jnp.sin(ang).reshape(shape)
    x1, x2, rest = x[..., :half], x[..., half:ROPE_DIMS], x[..., ROPE_DIMS:]
    return jnp.concatenate([x1 * cos - x2 * sin, x1 * sin + x2 * cos, rest], axis=-1)


def _token_mixer(x, positions, w_in, b_f, pe_cmp, w_cmp1, w_cmp2, w_gate, w_br, w_o):
    b_, s_, d_ = x.shape
    xt = x.reshape(b_ * s_, d_)
    proj = _mm(xt, w_in).reshape(b_, s_, IN_WIDTH)
    col = lambda name: proj[..., IN_OFFSETS[name]:IN_OFFSETS[name] + dict(IN_LAYOUT)[name]]
    hd = lambda t: t.reshape(b_, s_, -1, HEAD_DIM)
    heads_first = lambda t: jnp.moveaxis(t, 2, 1)
    as_q = lambda t: (heads_first(t) * ATTN_SCALE).astype(BF16)
    as_kv = lambda t: heads_first(t).astype(BF16)
    merge_heads = lambda o: jnp.moveaxis(o, 1, 2).reshape(b_ * s_, MIX_WIDTH)

    a_q = as_q(_rope(hd(col('a_q')), positions))
    cmp_kv = _compress(jnp.stack([col('a_k_cmp'), col('a_v_cmp')]), pe_cmp, w_cmp1, w_cmp2).astype(BF16)
    o_cmp, picked = _nsa_cmp_select(a_q, cmp_kv[0], cmp_kv[1])
    o_sel = _nsa_selected_attention(a_q, picked, _rope(col('a_k_sel'), positions).astype(BF16),
                                    col('a_v_sel').astype(BF16))
    gates = heads_first(jax.nn.sigmoid(col('a_gate').reshape(b_, s_, HEADS_PER_MIXER, 3)))
    o_a = _nsa_window_merge(a_q, _rope(col('a_k_win'), positions).astype(BF16), col('a_v_win').astype(BF16),
                            o_cmp, o_sel, gates)
    o_b = _stick_breaking_attention(as_q(hd(col('b_q'))), as_kv(hd(col('b_k'))), as_kv(hd(col('b_v'))))
    log_f = jax.nn.log_sigmoid(col('c_f') + b_f)
    cum = jnp.moveaxis(jnp.cumsum(log_f, axis=1), 1, 2)
    o_c = _fox_attention(as_q(hd(col('c_q'))), as_kv(hd(col('c_k'))), as_kv(hd(col('c_v'))), cum)
    d_iq = heads_first(_rope(col('d_iq').reshape(b_, s_, IDX_HEADS, IDX_DIM), positions)).astype(BF16)
    o_d = _dsa_attention(as_q(_rope(hd(col('d_q')), positions)), _rope(col('d_k'), positions).astype(BF16),
                         col('d_v').astype(BF16), d_iq, _rope(col('d_ik'), positions).astype(BF16), col('d_iw'))

    merged = sum(jax.nn.sigmoid(_mm(xt, w_gate[m])) * _mm(merge_heads(o), w_br[m])
                 for m, o in enumerate((o_a, o_b, o_c, o_d)))
    return _mm(merged, w_o).reshape(b_, s_, d_)


def _moe(x, w_router, router_bias, w_exp_gate, w_exp_up, w_exp_down, w_sh_gate, w_sh_up, w_sh_down):
    b_, s_, d_ = x.shape
    n_tok = b_ * s_
    xt = x.reshape(n_tok, d_)
    eidx, w, rank, counts = _router(xt, w_router, router_bias)
    n_assign = n_tok * TOP_K
    padded = (counts + MOE_BLOCK - 1) // MOE_BLOCK * MOE_BLOCK
    pad_end = jnp.cumsum(padded)
    dest = (pad_end - padded)[eidx] + rank
    n_slots = n_assign + N_EXPERTS * MOE_BLOCK
    n_blk = n_slots // MOE_BLOCK
    flat_tok = jnp.repeat(jnp.arange(n_tok, dtype=jnp.int32), TOP_K)
    slot_tok = jnp.full((n_slots,), n_tok, jnp.int32).at[dest.reshape(n_assign)].set(flat_tok)
    blk_expert = jnp.minimum(jnp.searchsorted(pad_end, jnp.arange(n_blk) * MOE_BLOCK, side='right'),
                             N_EXPERTS - 1)
    x_pad = jnp.concatenate([xt, jnp.zeros((1, d_), xt.dtype)], axis=0)
    xs = x_pad[slot_tok]
    y_slots = _expert_blocks(xs, blk_expert, w_exp_gate, w_exp_up, w_exp_down)
    routed = jnp.einsum('tk,tkd->td', w, y_slots[dest.reshape(n_assign)].reshape(n_tok, TOP_K, d_))
    sh = _mm(xt, jnp.concatenate([w_sh_gate, w_sh_up], axis=1))
    d_e = w_sh_gate.shape[1]
    shared = _mm(jax.nn.silu(sh[:, :d_e]) * sh[:, d_e:], w_sh_down)
    return (routed + shared).reshape(b_, s_, d_)


def kernel(x, positions, w_in, b_f, pe_cmp, w_cmp1, w_cmp2, w_gate, w_br, w_o, ln1_g, ln1_b, w_router, router_bias, w_exp_gate, w_exp_up, w_exp_down, w_sh_gate, w_sh_up, w_sh_down, ln2_g, ln2_b):
    depth = w_in.shape[0]
    alpha = (2 * depth) ** 0.25
    for l in range(depth):
        mix = _token_mixer(x, positions, w_in[l], b_f[l], pe_cmp[l], w_cmp1[l], w_cmp2[l],
                           w_gate[l], w_br[l], w_o[l])
        x = _layer_norm(alpha * x + mix, ln1_g[l], ln1_b[l])
        ffn = _moe(x, w_router[l], router_bias[l], w_exp_gate[l], w_exp_up[l], w_exp_down[l],
                   w_sh_gate[l], w_sh_up[l], w_sh_down[l])
        x = _layer_norm(alpha * x + ffn, ln2_g[l], ln2_b[l])
    return x
```

```python
import functools
import struct

import jax
import jax.numpy as jnp
import numpy as np
from jax import lax
from jax.experimental import pallas as pl
from jax.experimental.pallas import tpu as pltpu

HEAD_DIM = 64
HEADS_PER_MIXER = 4
MIX_WIDTH = HEADS_PER_MIXER * HEAD_DIM
N_MIXERS = 4
ROPE_THETA = 500000.0
ROPE_DIMS = HEAD_DIM // 4
CMP_LEN = 32
CMP_STRIDE = 16
SEL_BLOCK = 64
SEL_TOPN = 16
WINDOW = 512
FORCE_BONUS = 1.0e6
IDX_HEADS = 4
IDX_DIM = 64
DSA_TOPK_MAX = 256
N_EXPERTS = 64
TOP_K = 8
N_GROUPS = 8
TOPK_GROUPS = 4
ROUTE_SCALE = 2.5
MOE_BLOCK = 512
LN_EPS = 1e-5
NEG_INF = -1e30
TINY = 1e-30

IN_LAYOUT = (
    ('a_q', MIX_WIDTH), ('a_k_cmp', HEAD_DIM), ('a_v_cmp', HEAD_DIM), ('a_k_sel', HEAD_DIM),
    ('a_v_sel', HEAD_DIM), ('a_k_win', HEAD_DIM), ('a_v_win', HEAD_DIM), ('a_gate', 3 * HEADS_PER_MIXER),
    ('b_q', MIX_WIDTH), ('b_k', MIX_WIDTH), ('b_v', MIX_WIDTH),
    ('c_q', MIX_WIDTH), ('c_k', MIX_WIDTH), ('c_v', MIX_WIDTH), ('c_f', HEADS_PER_MIXER),
    ('d_q', MIX_WIDTH), ('d_k', HEAD_DIM), ('d_v', HEAD_DIM),
    ('d_iq', IDX_HEADS * IDX_DIM), ('d_ik', IDX_DIM), ('d_iw', IDX_HEADS),
)
IN_NAMES = tuple(n for n, _ in IN_LAYOUT)
IN_SIZES = tuple(c for _, c in IN_LAYOUT)
IN_WIDTH = sum(IN_SIZES)
IN_OFFSETS = dict(zip(IN_NAMES, np.cumsum((0,) + IN_SIZES[:-1]).tolist()))

LANES = 128
VMEM_LIMIT = 56 * 1024 * 1024
BF16 = jnp.bfloat16
F32 = jnp.float32
ATTN_SCALE = HEAD_DIM ** -0.5


def _round_up(n, m):
    return (n + m - 1) // m * m


def _f32_order_key(f):
    b = struct.unpack('<i', struct.pack('<f', f))[0]
    return b if b >= 0 else b ^ 0x7FFFFFFF


KEY_NEG_INF = _f32_order_key(NEG_INF)
INT_MIN = -2 ** 31


def _params(*sem):
    return pltpu.CompilerParams(dimension_semantics=sem, vmem_limit_bytes=VMEM_LIMIT)


def _keys_major(k):
    return jnp.swapaxes(k.astype(BF16), -1, -2)


def _mm_kernel(x_ref, w_ref, o_ref):
    o_ref[...] = jnp.dot(x_ref[...].astype(BF16), w_ref[...], preferred_element_type=F32)


def _mm(x, w, tm=512):
    m, k = x.shape
    n = w.shape[1]
    n_pad = _round_up(n, LANES)
    wb = w.astype(BF16)
    if n_pad != n:
        wb = jnp.pad(wb, ((0, 0), (0, n_pad - n)))
    tm = min(tm, m)
    out = pl.pallas_call(
        _mm_kernel,
        grid=(m // tm,),
        in_specs=[pl.BlockSpec((tm, k), lambda i: (i, 0)),
                  pl.BlockSpec((k, n_pad), lambda i: (0, 0))],
        out_specs=pl.BlockSpec((tm, n_pad), lambda i: (i, 0)),
        out_shape=jax.ShapeDtypeStruct((m, n_pad), F32),
        compiler_params=_params("parallel"),
        name="dense_proj",
    )(x, wb)
    return out[:, :n] if n_pad != n else out


def _flash_init(m_sc, acc_sc):
    m_sc[...] = jnp.full(m_sc.shape, NEG_INF, F32)
    acc_sc[...] = jnp.zeros(acc_sc.shape, F32)


def _flash_step(s, v_aug, m_sc, acc_sc):
    m_old = m_sc[...]
    m_new = jnp.maximum(m_old, jnp.max(s, axis=1, keepdims=True))
    p = jnp.concatenate([jnp.exp(s[:, c * LANES:(c + 1) * LANES] - m_new) for c in range(s.shape[1] // LANES)],
                        axis=1)
    acc_sc[...] = jnp.exp(m_old - m_new) * acc_sc[...] + jnp.dot(p.astype(BF16), v_aug,
                                                                  preferred_element_type=F32)
    m_sc[...] = m_new


def _flash_result(acc_sc):
    acc = acc_sc[...]
    return acc[:, :HEAD_DIM] / acc[:, HEAD_DIM:HEAD_DIM + 1]


def _augment_values(v):
    ones = jnp.ones(v.shape[:-1] + (1,), BF16)
    zeros = jnp.zeros(v.shape[:-1] + (LANES - v.shape[-1] - 1,), BF16)
    return jnp.concatenate([v.astype(BF16), ones, zeros], axis=-1)


def _tile_positions(q0, k0, tq, tk):
    qpos = q0 + lax.broadcasted_iota(jnp.int32, (tq, tk), 0)
    kpos = k0 + lax.broadcasted_iota(jnp.int32, (tq, tk), 1)
    return qpos, kpos


def _stack_mask(mask, heads):
    tq, tk = mask.shape
    return jnp.broadcast_to(mask[None], (heads, tq, tk)).reshape(heads * tq, tk)


N_SPLIT = 3


def _split_bf16(t):
    parts, rest = [], t
    for _ in range(N_SPLIT):
        bits = lax.bitcast_convert_type(rest, jnp.uint32) & jnp.uint32(0xFFFF0000)
        part = lax.bitcast_convert_type(bits, F32)
        parts.append(part.astype(BF16))
        rest = rest - part
    return jnp.stack(parts, axis=-1)


def _fox_kernel(q_ref, kt_ref, v_ref, o_ref, m_sc, acc_sc, *, tq, tk):
    q0 = pl.program_id(2) * tq
    q = q_ref[0, 0]
    _flash_init(m_sc, acc_sc)

    def step(j, masked):
        k0 = pl.multiple_of(j * tk, tk)
        s = jnp.dot(q, kt_ref[0, 0, :, pl.ds(k0, tk)], preferred_element_type=F32)
        if masked:
            qpos, kpos = _tile_positions(q0, k0, tq, tk)
            s = jnp.where(kpos <= qpos, s, NEG_INF)
        _flash_step(s, v_ref[0, 0, pl.ds(k0, tk), :], m_sc, acc_sc)

    n_full = q0 // tk

    def body(j, c):
        step(j, False)
        return c

    lax.fori_loop(0, n_full, body, 0)
    for d in range(tq // tk):
        step(n_full + d, True)
    o_ref[0, 0] = _flash_result(acc_sc)


def _fox_attention(q, k, v, cum, tq=512, tk=512):
    b, h, s, d = q.shape
    tq, tk = min(tq, s), min(tk, s)
    cum_terms = _split_bf16(cum)
    ones = jnp.ones_like(cum_terms)
    pad = jnp.zeros((b, h, s, LANES - d - 2 * N_SPLIT), BF16)
    q_aug = jnp.concatenate([q, cum_terms, ones, pad], axis=-1)
    k_aug = jnp.concatenate([k, ones, -cum_terms, pad], axis=-1)
    kern = functools.partial(_fox_kernel, tq=tq, tk=tk)
    return pl.pallas_call(
        kern,
        grid=(b, h, s // tq),
        in_specs=[pl.BlockSpec((1, 1, tq, LANES), lambda i, j, t: (i, j, t, 0)),
                  pl.BlockSpec((1, 1, LANES, s), lambda i, j, t: (i, j, 0, 0)),
                  pl.BlockSpec((1, 1, s, LANES), lambda i, j, t: (i, j, 0, 0))],
        out_specs=pl.BlockSpec((1, 1, tq, d), lambda i, j, t: (i, j, t, 0)),
        out_shape=jax.ShapeDtypeStruct((b, h, s, d), F32),
        scratch_shapes=[pltpu.VMEM((tq, LANES), F32), pltpu.VMEM((tq, LANES), F32)],
        compiler_params=_params("parallel", "parallel", "arbitrary"),
        name="forgetting_attention",
    )(q_aug, _keys_major(k_aug), _augment_values(v))


def _sb_kernel(q_ref, kt_ref, v_ref, tri_ref, o_ref, r_sc, acc_sc, *, tq, ck):
    q0 = pl.program_id(2) * tq
    q = q_ref[0, 0]
    tri = tri_ref[...]
    r_sc[...] = jnp.zeros(r_sc.shape, F32)
    acc_sc[...] = jnp.zeros(acc_sc.shape, F32)

    def chunk(c, masked):
        k0 = pl.multiple_of(c * ck, ck)
        vt = v_ref[0, 0, pl.ds(k0, ck), :]
        z = jnp.dot(q, kt_ref[0, 0, :, pl.ds(k0, ck)], preferred_element_type=F32)
        neg_z = -z
        log_fail = jnp.minimum(neg_z, 0.0) - jnp.log(1.0 + jnp.exp(jnp.minimum(z, neg_z)))
        if masked:
            qpos, kpos = _tile_positions(q0, k0, tq, ck)
            mask = kpos < qpos
            log_fail = jnp.where(mask, log_fail, 0.0)
        within = jnp.dot(log_fail.astype(BF16), tri, preferred_element_type=F32)
        r = r_sc[...]
        t = z + within
        a = jnp.concatenate([jnp.exp(t[:, c * LANES:(c + 1) * LANES] + r) for c in range(ck // LANES)], axis=1)
        if masked:
            a = jnp.where(mask, a, 0.0)
        acc_sc[...] += jnp.dot(a.astype(BF16), vt, preferred_element_type=F32)
        r_sc[...] = r + within[:, 0:1]

    n_diag = tq // ck
    n_below = q0 // ck
    for d in reversed(range(n_diag)):
        chunk(n_below + d, True)

    def body(i, c):
        base = n_below - (i + 1) * n_diag
        for d in reversed(range(n_diag)):
            chunk(base + d, False)
        return c

    lax.fori_loop(0, n_below // n_diag, body, 0)
    o_ref[0, 0] = acc_sc[...]


def _stick_breaking_attention(q, k, v, tq=512, ck=256):
    b, h, s, d = q.shape
    tq = min(tq, s)
    tri = jnp.asarray(np.tril(np.ones((ck, ck), np.float32)), dtype=BF16)
    kern = functools.partial(_sb_kernel, tq=tq, ck=ck)
    return pl.pallas_call(
        kern,
        grid=(b, h, s // tq),
        in_specs=[pl.BlockSpec((1, 1, tq, d), lambda i, j, t: (i, j, t, 0)),
                  pl.BlockSpec((1, 1, d, s), lambda i, j, t: (i, j, 0, 0)),
                  pl.BlockSpec((1, 1, s, d), lambda i, j, t: (i, j, 0, 0)),
                  pl.BlockSpec((ck, ck), lambda i, j, t: (0, 0))],
        out_specs=pl.BlockSpec((1, 1, tq, d), lambda i, j, t: (i, j, t, 0)),
        out_shape=jax.ShapeDtypeStruct((b, h, s, d), F32),
        scratch_shapes=[pltpu.VMEM((tq, LANES), F32), pltpu.VMEM((tq, d), F32)],
        compiler_params=_params("parallel", "parallel", "arbitrary"),
        name="stick_breaking_attention",
    )(q, _keys_major(k), v, tri)


def _dsa_kernel(iq_ref, iw_ref, ikt_ref, q_ref, kt_ref, v_ref, triu_ref, o_ref,
                key_sc, m_sc, acc_sc, *, tq, tk, topk, heads):
    q0 = pl.program_id(1) * tq
    n_kt = (q0 + tq + tk - 1) // tk
    per_tile = tk // LANES
    idx_scale = (IDX_DIM ** -0.5) * (IDX_HEADS ** -0.5)
    iw = iw_ref[0]
    iq = iq_ref[0].reshape(IDX_HEADS * tq, IDX_DIM)

    def score_tile(j, c):
        k0 = pl.multiple_of(j * tk, tk)
        rel = jnp.maximum(jnp.dot(iq, ikt_ref[0, :, pl.ds(k0, tk)], preferred_element_type=F32), 0.0)
        rel = rel.reshape(IDX_HEADS, tq, tk)
        score = iw[:, 0:1] * rel[0]
        for hh in range(1, IDX_HEADS):
            score = score + iw[:, hh:hh + 1] * rel[hh]
        score = score * idx_scale
        qpos, kpos = _tile_positions(q0, k0, tq, tk)
        score = jnp.where(score == 0.0, 0.0, score)
        score = jnp.where(kpos <= qpos, score, NEG_INF)
        bits = pltpu.bitcast(score, jnp.int32)
        key_sc[:, pl.ds(k0, tk)] = jnp.where(bits >= 0, bits, bits ^ 0x7FFFFFFF)
        return c

    lax.fori_loop(0, n_kt, score_tile, 0)

    count_rows = min(tq, 128)

    def count(pred, level, n=n_kt):
        parts = []
        for r0 in range(0, tq, count_rows):
            rows = slice(r0, r0 + count_rows)
            level_rows = jnp.broadcast_to(level[rows], (count_rows, LANES))

            def body(j, acc, rows=rows, level_rows=level_rows):
                k0 = pl.multiple_of(j * tk, tk)
                for u in range(per_tile):
                    acc = acc + jnp.where(pred(key_sc[rows, pl.ds(k0 + u * LANES, LANES)], level_rows), 1.0, 0.0)
                return acc
            acc = lax.fori_loop(0, n, body, jnp.zeros((count_rows, LANES), F32))
            parts.append(jnp.sum(acc, axis=1, keepdims=True))
        return jnp.concatenate(parts, axis=0)

    at_least = lambda kc, level: kc >= level
    c0 = count(at_least, jnp.zeros((tq, 1), jnp.int32))
    thr = jnp.where(c0 >= topk, 0, INT_MIN).astype(jnp.int32)
    cnt = jnp.where(c0 >= topk, c0, (n_kt * tk).astype(F32))

    def bit_step(bit, thr, cnt):
        cand = thr | lax.shift_left(jnp.int32(1), bit)
        c = count(at_least, cand)
        ok = c >= topk
        return jnp.where(ok, cand, thr), jnp.where(ok, c, cnt)

    thr, cnt = bit_step(jnp.int32(30), thr, cnt)
    steps_per_check = 3

    def unsettled(state):
        g, _, cnt = state
        return (g < 30 // steps_per_check) & (jnp.max(jnp.abs(cnt - topk)) > 0.0)

    def bit_group(state):
        g, thr, cnt = state
        for u in range(steps_per_check):
            thr, cnt = bit_step(29 - (g * steps_per_check + u), thr, cnt)
        return g + 1, thr, cnt

    _, thr, cnt = lax.while_loop(unsettled, bit_group, (jnp.int32(0), thr, cnt))
    tied = (thr > KEY_NEG_INF) & (cnt > topk)
    thr = jnp.maximum(thr, KEY_NEG_INF + 1)

    n_fix = jnp.where(jnp.max(jnp.where(tied, 1, 0)) > 0, n_kt, 0)
    n_gt = count(lambda kc, level: kc > level, thr, n_fix)
    room = topk - n_gt

    def fix(c, seen):
        sl = pl.ds(pl.multiple_of(c * LANES, LANES), LANES)
        kc = key_sc[:, sl]
        eq = kc == thr
        eqf = jnp.where(eq, 1.0, 0.0)
        incl = jnp.dot(eqf.astype(BF16), triu_ref[...], preferred_element_type=F32) + seen
        drop = eq & (incl - eqf >= room)
        key_sc[:, sl] = jnp.where(drop, kc - 1, kc)
        return seen + jnp.sum(eqf, axis=1, keepdims=True)

    lax.fori_loop(0, n_fix * per_tile, fix, jnp.zeros((tq, 1), F32))

    q = q_ref[0].reshape(heads * tq, HEAD_DIM)
    _flash_init(m_sc, acc_sc)

    def attn_tile(j, c):
        k0 = pl.multiple_of(j * tk, tk)
        mask = _stack_mask(key_sc[:, pl.ds(k0, tk)] >= thr, heads)
        s = jnp.dot(q, kt_ref[0, :, pl.ds(k0, tk)], preferred_element_type=F32)
        _flash_step(jnp.where(mask, s, NEG_INF), v_ref[0, pl.ds(k0, tk), :], m_sc, acc_sc)
        return c

    lax.fori_loop(0, n_kt, attn_tile, 0)
    o_ref[0] = _flash_result(acc_sc).reshape(heads, tq, HEAD_DIM)


def _dsa_attention(q, k, v, iq, ik, iw, tq=256, tk=512):
    b, h, s, d = q.shape
    tq, tk = min(tq, s), min(tk, s)
    topk = min(DSA_TOPK_MAX, s // 4)
    triu = jnp.asarray(np.triu(np.ones((LANES, LANES), np.float32)), dtype=BF16)
    kern = functools.partial(_dsa_kernel, tq=tq, tk=tk, topk=topk, heads=h)
    return pl.pallas_call(
        kern,
        grid=(b, s // tq),
        in_specs=[pl.BlockSpec((1, IDX_HEADS, tq, IDX_DIM), lambda i, t: (i, 0, t, 0)),
                  pl.BlockSpec((1, tq, IDX_HEADS), lambda i, t: (i, t, 0)),
                  pl.BlockSpec((1, IDX_DIM, s), lambda i, t: (i, 0, 0)),
                  pl.BlockSpec((1, h, tq, d), lambda i, t: (i, 0, t, 0)),
                  pl.BlockSpec((1, d, s), lambda i, t: (i, 0, 0)),
                  pl.BlockSpec((1, s, LANES), lambda i, t: (i, 0, 0)),
                  pl.BlockSpec((LANES, LANES), lambda i, t: (0, 0))],
        out_specs=pl.BlockSpec((1, h, tq, d), lambda i, t: (i, 0, t, 0)),
        out_shape=jax.ShapeDtypeStruct((b, h, s, d), F32),
        scratch_shapes=[pltpu.VMEM((tq, s), jnp.int32), pltpu.VMEM((h * tq, LANES), F32),
                        pltpu.VMEM((h * tq, LANES), F32)],
        compiler_params=_params("parallel", "arbitrary"),
        name="dsa_attention",
    )(iq, iw, _keys_major(ik), q, _keys_major(k), _augment_values(v), triu)


def _compress_kernel(x_ref, pe_ref, w1_ref, w2_ref, o_ref, *, half):
    x = x_ref[0, 0]
    n_rows = x.shape[0]
    ya = jnp.dot((x + pe_ref[0, :, :half]).astype(BF16), w1_ref[0, :half, :], preferred_element_type=F32)
    yb = jnp.dot((x + pe_ref[0, :, half:]).astype(BF16), w1_ref[0, half:, :], preferred_element_type=F32)
    hid = ya + pltpu.roll(yb, n_rows - 1, 0)
    o_ref[0, 0] = jnp.dot(jax.nn.gelu(hid).astype(BF16), w2_ref[0], preferred_element_type=F32)


def _compress(t2, pe_cmp, w_cmp1, w_cmp2):
    _, b, s, d = t2.shape
    assert CMP_LEN == 2 * CMP_STRIDE
    n_rows = s // CMP_STRIDE
    half = CMP_STRIDE * d
    x = t2.reshape(2, b, n_rows, half)
    pe = pe_cmp.reshape(2, 1, CMP_LEN * d)
    kern = functools.partial(_compress_kernel, half=half)
    return pl.pallas_call(
        kern,
        grid=(2, b),
        in_specs=[pl.BlockSpec((1, 1, n_rows, half), lambda j, i: (j, i, 0, 0)),
                  pl.BlockSpec((1, 1, CMP_LEN * d), lambda j, i: (j, 0, 0)),
                  pl.BlockSpec((1, CMP_LEN * d, d), lambda j, i: (j, 0, 0)),
                  pl.BlockSpec((1, d, d), lambda j, i: (j, 0, 0))],
        out_specs=pl.BlockSpec((1, 1, n_rows, d), lambda j, i: (j, i, 0, 0)),
        out_shape=jax.ShapeDtypeStruct((2, b, n_rows, d), F32),
        compiler_params=_params("parallel", "parallel"),
        name="nsa_compress",
    )(x, pe, w_cmp1.astype(BF16), w_cmp2.astype(BF16))


def _nsa_cmp_kernel(q_ref, kct_ref, vc_ref, ov_ref, o_ref, sel_ref, *, tq, heads, n_cmp, top_n):
    q0 = pl.program_id(1) * tq
    q = q_ref[0].reshape(heads * tq, HEAD_DIM)
    n_rows = kct_ref.shape[-1]
    n_sel = sel_ref.shape[-1]
    s = jnp.dot(q, kct_ref[0], preferred_element_type=F32)
    qpos, cid = _tile_positions(q0, 0, tq, n_rows)
    mask = _stack_mask((cid * CMP_STRIDE + (CMP_LEN - 1) <= qpos) & (cid < n_cmp), heads)
    s = jnp.where(mask, s, NEG_INF)
    p = jnp.where(mask, jnp.exp(s - jnp.max(s, axis=1, keepdims=True)), 0.0)
    p = p / jnp.maximum(jnp.sum(p, axis=1, keepdims=True), TINY)
    o_ref[0] = jnp.dot(p.astype(BF16), vc_ref[0], preferred_element_type=F32).reshape(heads, tq, HEAD_DIM)

    p_sum = jnp.sum(p.reshape(heads, tq, n_rows), axis=0)
    p_hi = p_sum.astype(BF16)
    p_lo = (p_sum - p_hi.astype(F32)).astype(BF16)
    imp = (jnp.dot(p_hi, ov_ref[...], preferred_element_type=F32)
           + jnp.dot(p_lo, ov_ref[...], preferred_element_type=F32))
    qrow, sid = _tile_positions(q0, 0, tq, n_sel)
    q_blk = qrow // SEL_BLOCK
    forced = (sid == 0) | (sid == q_blk) | (sid == q_blk - 1)
    imp = jnp.where(sid <= q_blk, imp + jnp.where(forced, FORCE_BONUS, 0.0), NEG_INF)

    sidf = sid.astype(F32)
    picked = jnp.zeros((tq, n_sel), F32)
    for _ in range(top_n):
        best = jnp.max(imp, axis=1, keepdims=True)
        first = jnp.min(jnp.where(imp == best, sidf, float(n_sel)), axis=1, keepdims=True)
        hit = sidf == first
        picked = jnp.where(hit, 1.0, picked)
        imp = jnp.where(hit, -3.0e38, imp)
    sel_ref[0] = picked.astype(BF16)


def _nsa_cmp_select(q, k_cmp, v_cmp, tq=128):
    b, h, s, d = q.shape
    tq = min(tq, s)
    n_rows = k_cmp.shape[1]
    n_cmp = (s - CMP_LEN) // CMP_STRIDE + 1
    n_sel = s // SEL_BLOCK
    top_n = min(SEL_TOPN, n_sel)
    c_lo = np.arange(n_rows)[:, None] * CMP_STRIDE
    s_lo = np.arange(n_sel)[None, :] * SEL_BLOCK
    overlap = ((c_lo < s_lo + SEL_BLOCK) & (c_lo + CMP_LEN - 1 >= s_lo) & (np.arange(n_rows)[:, None] < n_cmp))
    overlap = jnp.asarray(overlap.astype(np.float32), dtype=BF16)
    kern = functools.partial(_nsa_cmp_kernel, tq=tq, heads=h, n_cmp=n_cmp, top_n=top_n)
    return pl.pallas_call(
        kern,
        grid=(b, s // tq),
        in_specs=[pl.BlockSpec((1, h, tq, d), lambda i, t: (i, 0, t, 0)),
                  pl.BlockSpec((1, d, n_rows), lambda i, t: (i, 0, 0)),
                  pl.BlockSpec((1, n_rows, d), lambda i, t: (i, 0, 0)),
                  pl.BlockSpec((n_rows, n_sel), lambda i, t: (0, 0))],
        out_specs=[pl.BlockSpec((1, h, tq, d), lambda i, t: (i, 0, t, 0)),
                   pl.BlockSpec((1, tq, n_sel), lambda i, t: (i, t, 0))],
        out_shape=[jax.ShapeDtypeStruct((b, h, s, d), F32), jax.ShapeDtypeStruct((b, s, n_sel), BF16)],
        compiler_params=_params("parallel", "parallel"),
        name="nsa_compressed_select",
    )(q, _keys_major(k_cmp), v_cmp, overlap)


def _nsa_sel_kernel(q_ref, sel_ref, ex_ref, kt_ref, v_ref, o_ref, m_sc, acc_sc, *, tq, tk, heads):
    q0 = pl.program_id(1) * tq
    n_kt = (q0 + tq + tk - 1) // tk
    q = q_ref[0].reshape(heads * tq, HEAD_DIM)
    picked = sel_ref[0]
    _flash_init(m_sc, acc_sc)

    def tile(j, c):
        k0 = pl.multiple_of(j * tk, tk)
        in_block = jnp.dot(picked, ex_ref[:, pl.ds(k0, tk)], preferred_element_type=F32)
        qpos, kpos = _tile_positions(q0, k0, tq, tk)
        mask = _stack_mask((in_block > 0.5) & (kpos <= qpos), heads)
        s = jnp.dot(q, kt_ref[0, :, pl.ds(k0, tk)], preferred_element_type=F32)
        _flash_step(jnp.where(mask, s, NEG_INF), v_ref[0, pl.ds(k0, tk), :], m_sc, acc_sc)
        return c

    lax.fori_loop(0, n_kt, tile, 0)
    o_ref[0] = _flash_result(acc_sc).reshape(heads, tq, HEAD_DIM)


def _nsa_selected_attention(q, picked, k, v, tq=128, tk=512):
    b, h, s, d = q.shape
    tq, tk = min(tq, s), min(tk, s)
    n_sel = picked.shape[-1]
    expand = (np.arange(n_sel)[:, None] == np.arange(s)[None, :] // SEL_BLOCK).astype(np.float32)
    expand = jnp.asarray(expand, dtype=BF16)
    kern = functools.partial(_nsa_sel_kernel, tq=tq, tk=tk, heads=h)
    return pl.pallas_call(
        kern,
        grid=(b, s // tq),
        in_specs=[pl.BlockSpec((1, h, tq, d), lambda i, t: (i, 0, t, 0)),
                  pl.BlockSpec((1, tq, n_sel), lambda i, t: (i, t, 0)),
                  pl.BlockSpec((n_sel, s), lambda i, t: (0, 0)),
                  pl.BlockSpec((1, d, s), lambda i, t: (i, 0, 0)),
                  pl.BlockSpec((1, s, LANES), lambda i, t: (i, 0, 0))],
        out_specs=pl.BlockSpec((1, h, tq, d), lambda i, t: (i, 0, t, 0)),
        out_shape=jax.ShapeDtypeStruct((b, h, s, d), F32),
        scratch_shapes=[pltpu.VMEM((h * tq, LANES), F32), pltpu.VMEM((h * tq, LANES), F32)],
        compiler_params=_params("parallel", "arbitrary"),
        name="nsa_selected_attention",
    )(q, picked, expand, _keys_major(k), _augment_values(v))


def _nsa_win_kernel(q_ref, kt_ref, v_ref, oc_ref, os_ref, g_ref, o_ref, m_sc, acc_sc, *, tq, tk, heads):
    q0 = pl.program_id(1) * tq
    j_lo = jnp.maximum(q0 - WINDOW, 0) // tk
    j_hi = (q0 + tq) // tk
    q = q_ref[0].reshape(heads * tq, HEAD_DIM)
    _flash_init(m_sc, acc_sc)

    def tile(j, c):
        k0 = pl.multiple_of(j * tk, tk)
        qpos, kpos = _tile_positions(q0, k0, tq, tk)
        mask = _stack_mask((kpos <= qpos) & (kpos > qpos - WINDOW), heads)
        s = jnp.dot(q, kt_ref[0, :, pl.ds(k0, tk)], preferred_element_type=F32)
        _flash_step(jnp.where(mask, s, NEG_INF), v_ref[0, pl.ds(k0, tk), :], m_sc, acc_sc)
        return c

    lax.fori_loop(j_lo, j_hi, tile, 0)
    o_w = _flash_result(acc_sc).reshape(heads, tq, HEAD_DIM)
    g = g_ref[0]
    o_ref[0] = g[:, :, 0:1] * oc_ref[0] + g[:, :, 1:2] * os_ref[0] + g[:, :, 2:3] * o_w


def _nsa_window_merge(q, k, v, o_c, o_s, gates, tq=256, tk=256):
    b, h, s, d = q.shape
    tq, tk = min(tq, s), min(tk, s)
    assert tq % tk == 0 and WINDOW % tk == 0
    kern = functools.partial(_nsa_win_kernel, tq=tq, tk=tk, heads=h)
    head_spec = pl.BlockSpec((1, h, tq, d), lambda i, t: (i, 0, t, 0))
    return pl.pallas_call(
        kern,
        grid=(b, s // tq),
        in_specs=[head_spec,
                  pl.BlockSpec((1, d, s), lambda i, t: (i, 0, 0)),
                  pl.BlockSpec((1, s, LANES), lambda i, t: (i, 0, 0)),
                  head_spec, head_spec,
                  pl.BlockSpec((1, h, tq, 3), lambda i, t: (i, 0, t, 0))],
        out_specs=head_spec,
        out_shape=jax.ShapeDtypeStruct((b, h, s, d), F32),
        scratch_shapes=[pltpu.VMEM((h * tq, LANES), F32), pltpu.VMEM((h * tq, LANES), F32)],
        compiler_params=_params("parallel", "parallel"),
        name="nsa_window_merge",
    )(q, _keys_major(k), _augment_values(v), o_c, o_s, gates)


BIG = 3.0e38


def _first_max(v, lanef):
    best = jnp.max(v, axis=1, keepdims=True)
    first = jnp.min(jnp.where(v == best, lanef, float(LANES)), axis=1, keepdims=True)
    return best, first, lanef == first


def _router_kernel(x_ref, wr_ref, rb_ref, tri_ref, grp_ref, out_ref, cnt_ref, cnt_sc):
    @pl.when(pl.program_id(0) == 0)
    def _():
        cnt_sc[...] = jnp.zeros(cnt_sc.shape, F32)

    tm = x_ref.shape[0]
    per_group = N_EXPERTS // N_GROUPS
    scores = jax.nn.sigmoid(jnp.dot(x_ref[...].astype(BF16), wr_ref[...], preferred_element_type=F32))
    lane = lax.broadcasted_iota(jnp.int32, (tm, LANES), 1)
    lanef = lane.astype(F32)
    biased = jnp.where(lane < N_EXPERTS, scores + rb_ref[...], -BIG)

    grp_of_lane = lane // per_group
    gs = jnp.full((tm, LANES), -BIG, F32)
    for g in range(N_GROUPS):
        v = jnp.where(grp_of_lane == g, biased, -BIG)
        m1, _, hit = _first_max(v, lanef)
        m2 = jnp.max(jnp.where(hit, -BIG, v), axis=1, keepdims=True)
        gs = jnp.where(lane == g, m1 + m2, gs)
    gsel = jnp.zeros((tm, LANES), F32)
    for _ in range(TOPK_GROUPS):
        _, _, hit = _first_max(gs, lanef)
        gsel = jnp.where(hit, 1.0, gsel)
        gs = jnp.where(hit, -BIG, gs)
    emask = jnp.dot(gsel.astype(BF16), grp_ref[...], preferred_element_type=F32) > 0.5
    cand = jnp.where(lane < N_EXPERTS, jnp.where(emask, biased, NEG_INF), -BIG)

    picked = jnp.zeros((tm, LANES), F32)
    hits = []
    for _ in range(TOP_K):
        _, first, hit = _first_max(cand, lanef)
        hits.append((first, hit))
        picked = jnp.where(hit, 1.0, picked)
        cand = jnp.where(hit, -BIG, cand)
    w = scores * picked
    w = w / jnp.sum(w, axis=1, keepdims=True) * ROUTE_SCALE

    rank = jnp.dot(tri_ref[...], picked.astype(BF16), preferred_element_type=F32) + cnt_sc[...]
    cnt_sc[...] += jnp.sum(picked, axis=0, keepdims=True)
    cnt_ref[...] = cnt_sc[...]

    out = jnp.zeros((tm, LANES), F32)
    for k, (first, hit) in enumerate(hits):
        out = jnp.where(lane == k, first, out)
        out = jnp.where(lane == TOP_K + k, jnp.sum(jnp.where(hit, w, 0.0), axis=1, keepdims=True), out)
        out = jnp.where(lane == 2 * TOP_K + k, jnp.sum(jnp.where(hit, rank, 0.0), axis=1, keepdims=True), out)
    out_ref[...] = out


def _router(xt, w_router, router_bias, tm=512):
    n_tok, d = xt.shape
    tm = min(tm, n_tok)
    wr = jnp.pad(w_router.astype(BF16), ((0, 0), (0, LANES - N_EXPERTS)))
    rb = jnp.pad(router_bias.astype(F32), (0, LANES - N_EXPERTS)).reshape(1, LANES)
    tri = jnp.asarray(np.tril(np.ones((tm, tm), np.float32), -1), dtype=BF16)
    per_group = N_EXPERTS // N_GROUPS
    grp = (np.arange(LANES)[:, None] == np.arange(LANES)[None, :] // per_group) & (np.arange(LANES)[None, :] < N_EXPERTS)
    grp = jnp.asarray(grp.astype(np.float32), dtype=BF16)
    out, cnt = pl.pallas_call(
        _router_kernel,
        grid=(n_tok // tm,),
        in_specs=[pl.BlockSpec((tm, d), lambda i: (i, 0)),
                  pl.BlockSpec((d, LANES), lambda i: (0, 0)),
                  pl.BlockSpec((1, LANES), lambda i: (0, 0)),
                  pl.BlockSpec((tm, tm), lambda i: (0, 0)),
                  pl.BlockSpec((LANES, LANES), lambda i: (0, 0))],
        out_specs=[pl.BlockSpec((tm, LANES), lambda i: (i, 0)),
                   pl.BlockSpec((1, LANES), lambda i: (0, 0))],
        out_shape=[jax.ShapeDtypeStruct((n_tok, LANES), F32), jax.ShapeDtypeStruct((1, LANES), F32)],
        scratch_shapes=[pltpu.VMEM((1, LANES), F32)],
        compiler_params=_params("arbitrary"),
        name="moe_router",
    )(xt, wr, rb, tri, grp)
    eidx = out[:, :TOP_K].astype(jnp.int32)
    w = out[:, TOP_K:2 * TOP_K]
    rank = out[:, 2 * TOP_K:3 * TOP_K].astype(jnp.int32)
    return eidx, w, rank, cnt[0, :N_EXPERTS].astype(jnp.int32)


def _expert_kernel(be_ref, x_ref, wg_ref, wu_ref, wd_ref, o_ref):
    del be_ref
    xb = x_ref[...].astype(BF16)
    g = jnp.dot(xb, wg_ref[0].astype(BF16), preferred_element_type=F32)
    u = jnp.dot(xb, wu_ref[0].astype(BF16), preferred_element_type=F32)
    h = (g * jax.nn.sigmoid(g)) * u
    o_ref[...] = jnp.dot(h.astype(BF16), wd_ref[0].astype(BF16), preferred_element_type=F32)


def _expert_blocks(xs, blk_expert, w_g, w_u, w_d):
    n_slots, d = xs.shape
    d_e = w_g.shape[-1]
    n_blk = n_slots // MOE_BLOCK
    return pl.pallas_call(
        _expert_kernel,
        grid_spec=pltpu.PrefetchScalarGridSpec(
            num_scalar_prefetch=1,
            grid=(n_blk,),
            in_specs=[pl.BlockSpec((MOE_BLOCK, d), lambda i, be: (i, 0)),
                      pl.BlockSpec((1, d, d_e), lambda i, be: (be[i], 0, 0)),
                      pl.BlockSpec((1, d, d_e), lambda i, be: (be[i], 0, 0)),
                      pl.BlockSpec((1, d_e, d), lambda i, be: (be[i], 0, 0))],
            out_specs=pl.BlockSpec((MOE_BLOCK, d), lambda i, be: (i, 0))),
        out_shape=jax.ShapeDtypeStruct((n_slots, d), F32),
        compiler_params=_params("arbitrary"),
        name="routed_experts",
    )(blk_expert.astype(jnp.int32), xs, w_g, w_u, w_d)


def _layer_norm(x, g, b):
    mu = jnp.mean(x, axis=-1, keepdims=True)
    var = jnp.mean(jnp.square(x - mu), axis=-1, keepdims=True)
    return ((x - mu) * lax.rsqrt(var + LN_EPS)) * g + b


def _mixer_out_kernel(x_ref, oa_ref, ob_ref, oc_ref, od_ref, wg_ref, wbr_ref, wo_ref, g_ref, b_ref, o_ref, *, alpha):
    x = x_ref[...]
    xb = x.astype(BF16)
    merged = None
    for m, branch in enumerate((oa_ref, ob_ref, oc_ref, od_ref)):
        gate = jax.nn.sigmoid(jnp.dot(xb, wg_ref[m], preferred_element_type=F32))
        term = gate * jnp.dot(branch[...].astype(BF16), wbr_ref[m], preferred_element_type=F32)
        merged = term if merged is None else merged + term
    mix = jnp.dot(merged.astype(BF16), wo_ref[...], preferred_element_type=F32)
    o_ref[...] = _layer_norm(alpha * x + mix, g_ref[...], b_ref[...])


def _mixer_out(xt, branches, w_gate, w_br, w_o, ln_g, ln_b, alpha, tm=256):
    n_tok, d = xt.shape
    tm = min(tm, n_tok)
    width = branches[0].shape[1]
    row_spec = pl.BlockSpec((tm, d), lambda i: (i, 0))
    branch_spec = pl.BlockSpec((tm, width), lambda i: (i, 0))
    vec_spec = pl.BlockSpec((1, d), lambda i: (0, 0))
    kern = functools.partial(_mixer_out_kernel, alpha=alpha)
    return pl.pallas_call(
        kern,
        grid=(n_tok // tm,),
        in_specs=[row_spec, branch_spec, branch_spec, branch_spec, branch_spec,
                  pl.BlockSpec((N_MIXERS, d, d), lambda i: (0, 0, 0)),
                  pl.BlockSpec((N_MIXERS, width, d), lambda i: (0, 0, 0)),
                  pl.BlockSpec((d, d), lambda i: (0, 0)),
                  vec_spec, vec_spec],
        out_specs=row_spec,
        out_shape=jax.ShapeDtypeStruct((n_tok, d), F32),
        compiler_params=_params("parallel"),
        name="mixer_merge_out_norm",
    )(xt, *branches, w_gate.astype(BF16), w_br.astype(BF16), w_o.astype(BF16),
      ln_g.reshape(1, d), ln_b.reshape(1, d))


def _moe_out_kernel(x_ref, r_ref, wg_ref, wu_ref, wd_ref, g_ref, b_ref, o_ref, *, alpha):
    x = x_ref[...]
    xb = x.astype(BF16)
    gate = jnp.dot(xb, wg_ref[...], preferred_element_type=F32)
    up = jnp.dot(xb, wu_ref[...], preferred_element_type=F32)
    hidden = (gate * jax.nn.sigmoid(gate)) * up
    shared = jnp.dot(hidden.astype(BF16), wd_ref[...], preferred_element_type=F32)
    o_ref[...] = _layer_norm(alpha * x + (r_ref[...] + shared), g_ref[...], b_ref[...])


def _moe_out(xt, routed, w_sh_gate, w_sh_up, w_sh_down, ln_g, ln_b, alpha, tm=512):
    n_tok, d = xt.shape
    tm = min(tm, n_tok)
    d_e = w_sh_gate.shape[1]
    row_spec = pl.BlockSpec((tm, d), lambda i: (i, 0))
    vec_spec = pl.BlockSpec((1, d), lambda i: (0, 0))
    kern = functools.partial(_moe_out_kernel, alpha=alpha)
    return pl.pallas_call(
        kern,
        grid=(n_tok // tm,),
        in_specs=[row_spec, row_spec,
                  pl.BlockSpec((d, d_e), lambda i: (0, 0)), pl.BlockSpec((d, d_e), lambda i: (0, 0)),
                  pl.BlockSpec((d_e, d), lambda i: (0, 0)), vec_spec, vec_spec],
        out_specs=row_spec,
        out_shape=jax.ShapeDtypeStruct((n_tok, d), F32),
        compiler_params=_params("parallel"),
        name="moe_shared_out_norm",
    )(xt, routed, w_sh_gate.astype(BF16), w_sh_up.astype(BF16), w_sh_down.astype(BF16),
      ln_g.reshape(1, d), ln_b.reshape(1, d))


def _rope(x, positions):
    half = ROPE_DIMS // 2
    inv_freq = jnp.float32(ROPE_THETA) ** (-jnp.arange(half, dtype=F32) / half)
    ang = positions.astype(F32)[..., None] * inv_freq
    shape = ang.shape[:2] + (1,) * (x.ndim - 3) + (half,)
    cos = jnp.cos(ang).reshape(shape)
    sin = jnp.sin(ang).reshape(shape)
    x1, x2, rest = x[..., :half], x[..., half:ROPE_DIMS], x[..., ROPE_DIMS:]
    return jnp.concatenate([x1 * cos - x2 * sin, x1 * sin + x2 * cos, rest], axis=-1)


def _mixer_layer(x, positions, w_in, b_f, pe_cmp, w_cmp1, w_cmp2, w_gate, w_br, w_o, ln_g, ln_b, alpha):
    b_, s_, d_ = x.shape
    xt = x.reshape(b_ * s_, d_)
    proj = _mm(xt, w_in).reshape(b_, s_, IN_WIDTH)
    col = lambda name: proj[..., IN_OFFSETS[name]:IN_OFFSETS[name] + dict(IN_LAYOUT)[name]]
    hd = lambda t: t.reshape(b_, s_, -1, HEAD_DIM)
    heads_first = lambda t: jnp.moveaxis(t, 2, 1)
    as_q = lambda t: (heads_first(t) * ATTN_SCALE).astype(BF16)
    as_kv = lambda t: heads_first(t).astype(BF16)
    merge_heads = lambda o: jnp.moveaxis(o, 1, 2).reshape(b_ * s_, MIX_WIDTH)

    a_q = as_q(_rope(hd(col('a_q')), positions))
    cmp_kv = _compress(jnp.stack([col('a_k_cmp'), col('a_v_cmp')]), pe_cmp, w_cmp1, w_cmp2).astype(BF16)
    o_cmp, picked = _nsa_cmp_select(a_q, cmp_kv[0], cmp_kv[1])
    o_sel = _nsa_selected_attention(a_q, picked, _rope(col('a_k_sel'), positions).astype(BF16),
                                    col('a_v_sel').astype(BF16))
    gates = heads_first(jax.nn.sigmoid(col('a_gate').reshape(b_, s_, HEADS_PER_MIXER, 3)))
    o_a = _nsa_window_merge(a_q, _rope(col('a_k_win'), positions).astype(BF16), col('a_v_win').astype(BF16),
                            o_cmp, o_sel, gates)
    o_b = _stick_breaking_attention(as_q(hd(col('b_q'))), as_kv(hd(col('b_k'))), as_kv(hd(col('b_v'))))
    log_f = jax.nn.log_sigmoid(col('c_f') + b_f)
    cum = jnp.moveaxis(jnp.cumsum(log_f, axis=1), 1, 2)
    o_c = _fox_attention(as_q(hd(col('c_q'))), as_kv(hd(col('c_k'))), as_kv(hd(col('c_v'))), cum)
    d_iq = heads_first(_rope(col('d_iq').reshape(b_, s_, IDX_HEADS, IDX_DIM), positions)).astype(BF16)
    o_d = _dsa_attention(as_q(_rope(hd(col('d_q')), positions)), _rope(col('d_k'), positions).astype(BF16),
                         col('d_v').astype(BF16), d_iq, _rope(col('d_ik'), positions).astype(BF16), col('d_iw'))

    out = _mixer_out(xt, [merge_heads(o) for o in (o_a, o_b, o_c, o_d)], w_gate, w_br, w_o, ln_g, ln_b, alpha)
    return out.reshape(b_, s_, d_)


def _moe_layer(x, w_router, router_bias, w_exp_gate, w_exp_up, w_exp_down, w_sh_gate, w_sh_up, w_sh_down,
               ln_g, ln_b, alpha):
    b_, s_, d_ = x.shape
    n_tok = b_ * s_
    xt = x.reshape(n_tok, d_)
    eidx, w, rank, counts = _router(xt, w_router, router_bias)
    n_assign = n_tok * TOP_K
    padded = (counts + MOE_BLOCK - 1) // MOE_BLOCK * MOE_BLOCK
    pad_end = jnp.cumsum(padded)
    dest = (pad_end - padded)[eidx] + rank
    n_slots = n_assign + N_EXPERTS * MOE_BLOCK
    n_blk = n_slots // MOE_BLOCK
    flat_tok = jnp.repeat(jnp.arange(n_tok, dtype=jnp.int32), TOP_K)
    slot_tok = jnp.full((n_slots,), n_tok, jnp.int32).at[dest.reshape(n_assign)].set(flat_tok)
    blk_start = jnp.arange(n_blk, dtype=jnp.int32) * MOE_BLOCK
    blk_expert = jnp.minimum(jnp.sum(pad_end[None, :] <= blk_start[:, None], axis=1), N_EXPERTS - 1)
    x_pad = jnp.concatenate([xt, jnp.zeros((1, d_), xt.dtype)], axis=0)
    xs = x_pad[slot_tok]
    y_slots = _expert_blocks(xs, blk_expert, w_exp_gate, w_exp_up, w_exp_down)
    routed = jnp.einsum('tk,tkd->td', w, y_slots[dest.reshape(n_assign)].reshape(n_tok, TOP_K, d_))
    return _moe_out(xt, routed, w_sh_gate, w_sh_up, w_sh_down, ln_g, ln_b, alpha).reshape(b_, s_, d_)


def kernel(x, positions, w_in, b_f, pe_cmp, w_cmp1, w_cmp2, w_gate, w_br, w_o, ln1_g, ln1_b, w_router, router_bias, w_exp_gate, w_exp_up, w_exp_down, w_sh_gate, w_sh_up, w_sh_down, ln2_g, ln2_b):
    depth = w_in.shape[0]
    alpha = (2 * depth) ** 0.25
    for l in range(depth):
        x = _mixer_layer(x, positions, w_in[l], b_f[l], pe_cmp[l], w_cmp1[l], w_cmp2[l],
                         w_gate[l], w_br[l], w_o[l], ln1_g[l], ln1_b[l], alpha)
        x = _moe_layer(x, w_router[l], router_bias[l], w_exp_gate[l], w_exp_up[l], w_exp_down[l],
                       w_sh_gate[l], w_sh_up[l], w_sh_down[l], ln2_g[l], ln2_b[l], alpha)
    return x
```

```python
import functools
import struct

import jax
import jax.numpy as jnp
import numpy as np
from jax import lax
from jax.experimental import pallas as pl
from jax.experimental.pallas import tpu as pltpu

HEAD_DIM = 64
HEADS_PER_MIXER = 4
MIX_WIDTH = HEADS_PER_MIXER * HEAD_DIM
N_MIXERS = 4
ROPE_THETA = 500000.0
ROPE_DIMS = HEAD_DIM // 4
CMP_LEN = 32
CMP_STRIDE = 16
SEL_BLOCK = 64
SEL_TOPN = 16
WINDOW = 512
FORCE_BONUS = 1.0e6
IDX_HEADS = 4
IDX_DIM = 64
DSA_TOPK_MAX = 256
N_EXPERTS = 64
TOP_K = 8
N_GROUPS = 8
TOPK_GROUPS = 4
ROUTE_SCALE = 2.5
MOE_BLOCK = 512
LN_EPS = 1e-5
NEG_INF = -1e30
TINY = 1e-30

IN_LAYOUT = (
    ('a_q', MIX_WIDTH), ('a_k_cmp', HEAD_DIM), ('a_v_cmp', HEAD_DIM), ('a_k_sel', HEAD_DIM),
    ('a_v_sel', HEAD_DIM), ('a_k_win', HEAD_DIM), ('a_v_win', HEAD_DIM), ('a_gate', 3 * HEADS_PER_MIXER),
    ('b_q', MIX_WIDTH), ('b_k', MIX_WIDTH), ('b_v', MIX_WIDTH),
    ('c_q', MIX_WIDTH), ('c_k', MIX_WIDTH), ('c_v', MIX_WIDTH), ('c_f', HEADS_PER_MIXER),
    ('d_q', MIX_WIDTH), ('d_k', HEAD_DIM), ('d_v', HEAD_DIM),
    ('d_iq', IDX_HEADS * IDX_DIM), ('d_ik', IDX_DIM), ('d_iw', IDX_HEADS),
)
IN_NAMES = tuple(n for n, _ in IN_LAYOUT)
IN_SIZES = tuple(c for _, c in IN_LAYOUT)
IN_WIDTH = sum(IN_SIZES)
IN_OFFSETS = dict(zip(IN_NAMES, np.cumsum((0,) + IN_SIZES[:-1]).tolist()))

LANES = 128
VMEM_LIMIT = 56 * 1024 * 1024
BF16 = jnp.bfloat16
F32 = jnp.float32
ATTN_SCALE = HEAD_DIM ** -0.5


def _round_up(n, m):
    return (n + m - 1) // m * m


def _f32_order_key(f):
    b = struct.unpack('<i', struct.pack('<f', f))[0]
    return b if b >= 0 else b ^ 0x7FFFFFFF


KEY_NEG_INF = _f32_order_key(NEG_INF)
INT_MIN = -2 ** 31


def _params(*sem):
    return pltpu.CompilerParams(dimension_semantics=sem, vmem_limit_bytes=VMEM_LIMIT)


def _keys_major(k):
    return jnp.swapaxes(k.astype(BF16), -1, -2)


def _mm_kernel(x_ref, w_ref, o_ref):
    o_ref[...] = jnp.dot(x_ref[...].astype(BF16), w_ref[...], preferred_element_type=F32)


def _mm(x, w, tm=512):
    m, k = x.shape
    n = w.shape[1]
    n_pad = _round_up(n, LANES)
    wb = w.astype(BF16)
    if n_pad != n:
        wb = jnp.pad(wb, ((0, 0), (0, n_pad - n)))
    tm = min(tm, m)
    out = pl.pallas_call(
        _mm_kernel,
        grid=(m // tm,),
        in_specs=[pl.BlockSpec((tm, k), lambda i: (i, 0)),
                  pl.BlockSpec((k, n_pad), lambda i: (0, 0))],
        out_specs=pl.BlockSpec((tm, n_pad), lambda i: (i, 0)),
        out_shape=jax.ShapeDtypeStruct((m, n_pad), F32),
        compiler_params=_params("parallel"),
        name="dense_proj",
    )(x, wb)
    return out[:, :n] if n_pad != n else out


def _flash_init(m_sc, acc_sc):
    m_sc[...] = jnp.full(m_sc.shape, NEG_INF, F32)
    acc_sc[...] = jnp.zeros(acc_sc.shape, F32)


def _flash_step(s, v_aug, m_sc, acc_sc):
    m_old = m_sc[...]
    m_new = jnp.maximum(m_old, jnp.max(s, axis=1, keepdims=True))
    p = jnp.concatenate([jnp.exp(s[:, c * LANES:(c + 1) * LANES] - m_new) for c in range(s.shape[1] // LANES)],
                        axis=1)
    acc_sc[...] = jnp.exp(m_old - m_new) * acc_sc[...] + jnp.dot(p.astype(BF16), v_aug,
                                                                  preferred_element_type=F32)
    m_sc[...] = m_new


def _flash_result(acc_sc):
    acc = acc_sc[...]
    return acc[:, :HEAD_DIM] / acc[:, HEAD_DIM:HEAD_DIM + 1]


def _augment_values(v):
    ones = jnp.ones(v.shape[:-1] + (1,), BF16)
    zeros = jnp.zeros(v.shape[:-1] + (LANES - v.shape[-1] - 1,), BF16)
    return jnp.concatenate([v.astype(BF16), ones, zeros], axis=-1)


def _tile_positions(q0, k0, tq, tk):
    qpos = q0 + lax.broadcasted_iota(jnp.int32, (tq, tk), 0)
    kpos = k0 + lax.broadcasted_iota(jnp.int32, (tq, tk), 1)
    return qpos, kpos


def _stack_mask(mask, heads):
    tq, tk = mask.shape
    return jnp.broadcast_to(mask[None], (heads, tq, tk)).reshape(heads * tq, tk)


N_SPLIT = 3


def _split_bf16(t):
    parts, rest = [], t
    for _ in range(N_SPLIT):
        bits = lax.bitcast_convert_type(rest, jnp.uint32) & jnp.uint32(0xFFFF0000)
        part = lax.bitcast_convert_type(bits, F32)
        parts.append(part.astype(BF16))
        rest = rest - part
    return jnp.stack(parts, axis=-1)


def _fox_kernel(q_ref, kt_ref, v_ref, o_ref, m_sc, acc_sc, *, tq, tk):
    q0 = pl.program_id(2) * tq
    q = q_ref[0, 0]
    _flash_init(m_sc, acc_sc)

    def step(j, masked):
        k0 = pl.multiple_of(j * tk, tk)
        s = jnp.dot(q, kt_ref[0, 0, :, pl.ds(k0, tk)], preferred_element_type=F32)
        if masked:
            qpos, kpos = _tile_positions(q0, k0, tq, tk)
            s = jnp.where(kpos <= qpos, s, NEG_INF)
        _flash_step(s, v_ref[0, 0, pl.ds(k0, tk), :], m_sc, acc_sc)

    n_full = q0 // tk

    def body(j, c):
        step(j, False)
        return c

    lax.fori_loop(0, n_full, body, 0)
    for d in range(tq // tk):
        step(n_full + d, True)
    o_ref[0, 0] = _flash_result(acc_sc)


def _fox_attention(q, k, v, cum, tq=512, tk=512):
    b, h, s, d = q.shape
    tq, tk = min(tq, s), min(tk, s)
    cum_terms = _split_bf16(cum)
    ones = jnp.ones_like(cum_terms)
    pad = jnp.zeros((b, h, s, LANES - d - 2 * N_SPLIT), BF16)
    q_aug = jnp.concatenate([q, cum_terms, ones, pad], axis=-1)
    k_aug = jnp.concatenate([k, ones, -cum_terms, pad], axis=-1)
    kern = functools.partial(_fox_kernel, tq=tq, tk=tk)
    return pl.pallas_call(
        kern,
        grid=(b, h, s // tq),
        in_specs=[pl.BlockSpec((1, 1, tq, LANES), lambda i, j, t: (i, j, t, 0)),
                  pl.BlockSpec((1, 1, LANES, s), lambda i, j, t: (i, j, 0, 0)),
                  pl.BlockSpec((1, 1, s, LANES), lambda i, j, t: (i, j, 0, 0))],
        out_specs=pl.BlockSpec((1, 1, tq, d), lambda i, j, t: (i, j, t, 0)),
        out_shape=jax.ShapeDtypeStruct((b, h, s, d), F32),
        scratch_shapes=[pltpu.VMEM((tq, LANES), F32), pltpu.VMEM((tq, LANES), F32)],
        compiler_params=_params("parallel", "parallel", "arbitrary"),
        name="forgetting_attention",
    )(q_aug, _keys_major(k_aug), _augment_values(v))


def _sb_kernel(q_ref, kt_ref, v_ref, tri_ref, o_ref, r_sc, acc_sc, *, tq, ck):
    q0 = pl.program_id(2) * tq
    q = q_ref[0, 0]
    tri = tri_ref[...]
    r_sc[...] = jnp.zeros(r_sc.shape, F32)
    acc_sc[...] = jnp.zeros(acc_sc.shape, F32)

    def chunk(c, masked):
        k0 = pl.multiple_of(c * ck, ck)
        vt = v_ref[0, 0, pl.ds(k0, ck), :]
        z = jnp.dot(q, kt_ref[0, 0, :, pl.ds(k0, ck)], preferred_element_type=F32)
        neg_z = -z
        log_fail = jnp.minimum(neg_z, 0.0) - jnp.log(1.0 + jnp.exp(jnp.minimum(z, neg_z)))
        if masked:
            qpos, kpos = _tile_positions(q0, k0, tq, ck)
            mask = kpos < qpos
            log_fail = jnp.where(mask, log_fail, 0.0)
        within = jnp.dot(log_fail.astype(BF16), tri, preferred_element_type=F32)
        r = r_sc[...]
        t = z + within
        a = jnp.concatenate([jnp.exp(t[:, c * LANES:(c + 1) * LANES] + r) for c in range(ck // LANES)], axis=1)
        if masked:
            a = jnp.where(mask, a, 0.0)
        acc_sc[...] += jnp.dot(a.astype(BF16), vt, preferred_element_type=F32)
        r_sc[...] = r + within[:, 0:1]

    n_diag = tq // ck
    n_below = q0 // ck
    for d in reversed(range(n_diag)):
        chunk(n_below + d, True)

    def body(i, c):
        base = n_below - (i + 1) * n_diag
        for d in reversed(range(n_diag)):
            chunk(base + d, False)
        return c

    lax.fori_loop(0, n_below // n_diag, body, 0)
    o_ref[0, 0] = acc_sc[...]


def _stick_breaking_attention(q, k, v, tq=512, ck=256):
    b, h, s, d = q.shape
    tq = min(tq, s)
    tri = jnp.asarray(np.tril(np.ones((ck, ck), np.float32)), dtype=BF16)
    kern = functools.partial(_sb_kernel, tq=tq, ck=ck)
    return pl.pallas_call(
        kern,
        grid=(b, h, s // tq),
        in_specs=[pl.BlockSpec((1, 1, tq, d), lambda i, j, t: (i, j, t, 0)),
                  pl.BlockSpec((1, 1, d, s), lambda i, j, t: (i, j, 0, 0)),
                  pl.BlockSpec((1, 1, s, d), lambda i, j, t: (i, j, 0, 0)),
                  pl.BlockSpec((ck, ck), lambda i, j, t: (0, 0))],
        out_specs=pl.BlockSpec((1, 1, tq, d), lambda i, j, t: (i, j, t, 0)),
        out_shape=jax.ShapeDtypeStruct((b, h, s, d), F32),
        scratch_shapes=[pltpu.VMEM((tq, LANES), F32), pltpu.VMEM((tq, d), F32)],
        compiler_params=_params("parallel", "parallel", "arbitrary"),
        name="stick_breaking_attention",
    )(q, _keys_major(k), v, tri)


def _dsa_kernel(iq_ref, iw_ref, ikt_ref, q_ref, kt_ref, v_ref, triu_ref, o_ref,
                key_sc, m_sc, acc_sc, *, tq, tk, topk, heads):
    q0 = pl.program_id(1) * tq
    n_kt = (q0 + tq + tk - 1) // tk
    per_tile = tk // LANES
    idx_scale = (IDX_DIM ** -0.5) * (IDX_HEADS ** -0.5)
    iw = iw_ref[0]
    iq = iq_ref[0].reshape(IDX_HEADS * tq, IDX_DIM)

    def score_tile(j, c):
        k0 = pl.multiple_of(j * tk, tk)
        rel = jnp.maximum(jnp.dot(iq, ikt_ref[0, :, pl.ds(k0, tk)], preferred_element_type=F32), 0.0)
        rel = rel.reshape(IDX_HEADS, tq, tk)
        score = iw[:, 0:1] * rel[0]
        for hh in range(1, IDX_HEADS):
            score = score + iw[:, hh:hh + 1] * rel[hh]
        score = score * idx_scale
        qpos, kpos = _tile_positions(q0, k0, tq, tk)
        score = jnp.where(score == 0.0, 0.0, score)
        score = jnp.where(kpos <= qpos, score, NEG_INF)
        bits = pltpu.bitcast(score, jnp.int32)
        key_sc[:, pl.ds(k0, tk)] = jnp.where(bits >= 0, bits, bits ^ 0x7FFFFFFF)
        return c

    lax.fori_loop(0, n_kt, score_tile, 0)

    count_rows = min(tq, 128)

    def count(pred, level, n=n_kt):
        parts = []
        for r0 in range(0, tq, count_rows):
            rows = slice(r0, r0 + count_rows)
            level_rows = jnp.broadcast_to(level[rows], (count_rows, LANES))

            def body(j, acc, rows=rows, level_rows=level_rows):
                k0 = pl.multiple_of(j * tk, tk)
                for u in range(per_tile):
                    acc = acc + jnp.where(pred(key_sc[rows, pl.ds(k0 + u * LANES, LANES)], level_rows), 1.0, 0.0)
                return acc
            acc = lax.fori_loop(0, n, body, jnp.zeros((count_rows, LANES), F32))
            parts.append(jnp.sum(acc, axis=1, keepdims=True))
        return jnp.concatenate(parts, axis=0)

    at_least = lambda kc, level: kc >= level
    c0 = count(at_least, jnp.zeros((tq, 1), jnp.int32))
    thr = jnp.where(c0 >= topk, 0, INT_MIN).astype(jnp.int32)
    cnt = jnp.where(c0 >= topk, c0, (n_kt * tk).astype(F32))

    def bit_step(bit, thr, cnt):
        cand = thr | lax.shift_left(jnp.int32(1), bit)
        c = count(at_least, cand)
        ok = c >= topk
        return jnp.where(ok, cand, thr), jnp.where(ok, c, cnt)

    thr, cnt = bit_step(jnp.int32(30), thr, cnt)
    steps_per_check = 3

    def unsettled(state):
        g, _, cnt = state
        return (g < 30 // steps_per_check) & (jnp.max(jnp.abs(cnt - topk)) > 0.0)

    def bit_group(state):
        g, thr, cnt = state
        for u in range(steps_per_check):
            thr, cnt = bit_step(29 - (g * steps_per_check + u), thr, cnt)
        return g + 1, thr, cnt

    _, thr, cnt = lax.while_loop(unsettled, bit_group, (jnp.int32(0), thr, cnt))
    tied = (thr > KEY_NEG_INF) & (cnt > topk)
    thr = jnp.maximum(thr, KEY_NEG_INF + 1)

    n_fix = jnp.where(jnp.max(jnp.where(tied, 1, 0)) > 0, n_kt, 0)
    n_gt = count(lambda kc, level: kc > level, thr, n_fix)
    room = topk - n_gt

    def fix(c, seen):
        sl = pl.ds(pl.multiple_of(c * LANES, LANES), LANES)
        kc = key_sc[:, sl]
        eq = kc == thr
        eqf = jnp.where(eq, 1.0, 0.0)
        incl = jnp.dot(eqf.astype(BF16), triu_ref[...], preferred_element_type=F32) + seen
        drop = eq & (incl - eqf >= room)
        key_sc[:, sl] = jnp.where(drop, kc - 1, kc)
        return seen + jnp.sum(eqf, axis=1, keepdims=True)

    lax.fori_loop(0, n_fix * per_tile, fix, jnp.zeros((tq, 1), F32))

    q = q_ref[0].reshape(heads * tq, HEAD_DIM)
    _flash_init(m_sc, acc_sc)

    def attn_tile(j, c):
        k0 = pl.multiple_of(j * tk, tk)
        mask = _stack_mask(key_sc[:, pl.ds(k0, tk)] >= thr, heads)
        s = jnp.dot(q, kt_ref[0, :, pl.ds(k0, tk)], preferred_element_type=F32)
        _flash_step(jnp.where(mask, s, NEG_INF), v_ref[0, pl.ds(k0, tk), :], m_sc, acc_sc)
        return c

    lax.fori_loop(0, n_kt, attn_tile, 0)
    o_ref[0] = _flash_result(acc_sc).reshape(heads, tq, HEAD_DIM)


def _dsa_attention(q, k, v, iq, ik, iw, tq=256, tk=512):
    b, h, s, d = q.shape
    tq, tk = min(tq, s), min(tk, s)
    topk = min(DSA_TOPK_MAX, s // 4)
    triu = jnp.asarray(np.triu(np.ones((LANES, LANES), np.float32)), dtype=BF16)
    kern = functools.partial(_dsa_kernel, tq=tq, tk=tk, topk=topk, heads=h)
    return pl.pallas_call(
        kern,
        grid=(b, s // tq),
        in_specs=[pl.BlockSpec((1, IDX_HEADS, tq, IDX_DIM), lambda i, t: (i, 0, t, 0)),
                  pl.BlockSpec((1, tq, IDX_HEADS), lambda i, t: (i, t, 0)),
                  pl.BlockSpec((1, IDX_DIM, s), lambda i, t: (i, 0, 0)),
                  pl.BlockSpec((1, h, tq, d), lambda i, t: (i, 0, t, 0)),
                  pl.BlockSpec((1, d, s), lambda i, t: (i, 0, 0)),
                  pl.BlockSpec((1, s, LANES), lambda i, t: (i, 0, 0)),
                  pl.BlockSpec((LANES, LANES), lambda i, t: (0, 0))],
        out_specs=pl.BlockSpec((1, h, tq, d), lambda i, t: (i, 0, t, 0)),
        out_shape=jax.ShapeDtypeStruct((b, h, s, d), F32),
        scratch_shapes=[pltpu.VMEM((tq, s), jnp.int32), pltpu.VMEM((h * tq, LANES), F32),
                        pltpu.VMEM((h * tq, LANES), F32)],
        compiler_params=_params("parallel", "arbitrary"),
        name="dsa_attention",
    )(iq, iw, _keys_major(ik), q, _keys_major(k), _augment_values(v), triu)


def _compress_kernel(x_ref, pe_ref, w1_ref, w2_ref, o_ref, *, half):
    x = x_ref[0, 0]
    n_rows = x.shape[0]
    ya = jnp.dot((x + pe_ref[0, :, :half]).astype(BF16), w1_ref[0, :half, :], preferred_element_type=F32)
    yb = jnp.dot((x + pe_ref[0, :, half:]).astype(BF16), w1_ref[0, half:, :], preferred_element_type=F32)
    hid = ya + pltpu.roll(yb, n_rows - 1, 0)
    o_ref[0, 0] = jnp.dot(jax.nn.gelu(hid).astype(BF16), w2_ref[0], preferred_element_type=F32)


def _compress(t2, pe_cmp, w_cmp1, w_cmp2):
    _, b, s, d = t2.shape
    assert CMP_LEN == 2 * CMP_STRIDE
    n_rows = s // CMP_STRIDE
    half = CMP_STRIDE * d
    x = t2.reshape(2, b, n_rows, half)
    pe = pe_cmp.reshape(2, 1, CMP_LEN * d)
    kern = functools.partial(_compress_kernel, half=half)
    return pl.pallas_call(
        kern,
        grid=(2, b),
        in_specs=[pl.BlockSpec((1, 1, n_rows, half), lambda j, i: (j, i, 0, 0)),
                  pl.BlockSpec((1, 1, CMP_LEN * d), lambda j, i: (j, 0, 0)),
                  pl.BlockSpec((1, CMP_LEN * d, d), lambda j, i: (j, 0, 0)),
                  pl.BlockSpec((1, d, d), lambda j, i: (j, 0, 0))],
        out_specs=pl.BlockSpec((1, 1, n_rows, d), lambda j, i: (j, i, 0, 0)),
        out_shape=jax.ShapeDtypeStruct((2, b, n_rows, d), F32),
        compiler_params=_params("parallel", "parallel"),
        name="nsa_compress",
    )(x, pe, w_cmp1.astype(BF16), w_cmp2.astype(BF16))


def _nsa_cmp_kernel(q_ref, kct_ref, vc_ref, ov_ref, o_ref, sel_ref, *, tq, heads, n_cmp, top_n):
    q0 = pl.program_id(1) * tq
    q = q_ref[0].reshape(heads * tq, HEAD_DIM)
    n_rows = kct_ref.shape[-1]
    n_sel = sel_ref.shape[-1]
    s = jnp.dot(q, kct_ref[0], preferred_element_type=F32)
    qpos, cid = _tile_positions(q0, 0, tq, n_rows)
    mask = _stack_mask((cid * CMP_STRIDE + (CMP_LEN - 1) <= qpos) & (cid < n_cmp), heads)
    s = jnp.where(mask, s, NEG_INF)
    p = jnp.where(mask, jnp.exp(s - jnp.max(s, axis=1, keepdims=True)), 0.0)
    p = p / jnp.maximum(jnp.sum(p, axis=1, keepdims=True), TINY)
    o_ref[0] = jnp.dot(p.astype(BF16), vc_ref[0], preferred_element_type=F32).reshape(heads, tq, HEAD_DIM)

    p_sum = jnp.sum(p.reshape(heads, tq, n_rows), axis=0)
    p_hi = p_sum.astype(BF16)
    p_lo = (p_sum - p_hi.astype(F32)).astype(BF16)
    imp = (jnp.dot(p_hi, ov_ref[...], preferred_element_type=F32)
           + jnp.dot(p_lo, ov_ref[...], preferred_element_type=F32))
    qrow, sid = _tile_positions(q0, 0, tq, n_sel)
    q_blk = qrow // SEL_BLOCK
    forced = (sid == 0) | (sid == q_blk) | (sid == q_blk - 1)
    imp = jnp.where(sid <= q_blk, imp + jnp.where(forced, FORCE_BONUS, 0.0), NEG_INF)

    sidf = sid.astype(F32)
    picked = jnp.zeros((tq, n_sel), F32)
    for _ in range(top_n):
        best = jnp.max(imp, axis=1, keepdims=True)
        first = jnp.min(jnp.where(imp == best, sidf, float(n_sel)), axis=1, keepdims=True)
        hit = sidf == first
        picked = jnp.where(hit, 1.0, picked)
        imp = jnp.where(hit, -3.0e38, imp)
    sel_ref[0] = picked.astype(BF16)


def _nsa_cmp_select(q, k_cmp, v_cmp, tq=128):
    b, h, s, d = q.shape
    tq = min(tq, s)
    n_rows = k_cmp.shape[1]
    n_cmp = (s - CMP_LEN) // CMP_STRIDE + 1
    n_sel = s // SEL_BLOCK
    top_n = min(SEL_TOPN, n_sel)
    c_lo = np.arange(n_rows)[:, None] * CMP_STRIDE
    s_lo = np.arange(n_sel)[None, :] * SEL_BLOCK
    overlap = ((c_lo < s_lo + SEL_BLOCK) & (c_lo + CMP_LEN - 1 >= s_lo) & (np.arange(n_rows)[:, None] < n_cmp))
    overlap = jnp.asarray(overlap.astype(np.float32), dtype=BF16)
    kern = functools.partial(_nsa_cmp_kernel, tq=tq, heads=h, n_cmp=n_cmp, top_n=top_n)
    return pl.pallas_call(
        kern,
        grid=(b, s // tq),
        in_specs=[pl.BlockSpec((1, h, tq, d), lambda i, t: (i, 0, t, 0)),
                  pl.BlockSpec((1, d, n_rows), lambda i, t: (i, 0, 0)),
                  pl.BlockSpec((1, n_rows, d), lambda i, t: (i, 0, 0)),
                  pl.BlockSpec((n_rows, n_sel), lambda i, t: (0, 0))],
        out_specs=[pl.BlockSpec((1, h, tq, d), lambda i, t: (i, 0, t, 0)),
                   pl.BlockSpec((1, tq, n_sel), lambda i, t: (i, t, 0))],
        out_shape=[jax.ShapeDtypeStruct((b, h, s, d), F32), jax.ShapeDtypeStruct((b, s, n_sel), BF16)],
        compiler_params=_params("parallel", "parallel"),
        name="nsa_compressed_select",
    )(q, _keys_major(k_cmp), v_cmp, overlap)


def _nsa_sel_kernel(q_ref, sel_ref, ex_ref, kt_ref, v_ref, o_ref, m_sc, acc_sc, *, tq, tk, heads):
    q0 = pl.program_id(1) * tq
    n_kt = (q0 + tq + tk - 1) // tk
    q = q_ref[0].reshape(heads * tq, HEAD_DIM)
    picked = sel_ref[0]
    _flash_init(m_sc, acc_sc)

    def tile(j, c):
        k0 = pl.multiple_of(j * tk, tk)
        in_block = jnp.dot(picked, ex_ref[:, pl.ds(k0, tk)], preferred_element_type=F32)
        qpos, kpos = _tile_positions(q0, k0, tq, tk)
        mask = _stack_mask((in_block > 0.5) & (kpos <= qpos), heads)
        s = jnp.dot(q, kt_ref[0, :, pl.ds(k0, tk)], preferred_element_type=F32)
        _flash_step(jnp.where(mask, s, NEG_INF), v_ref[0, pl.ds(k0, tk), :], m_sc, acc_sc)
        return c

    lax.fori_loop(0, n_kt, tile, 0)
    o_ref[0] = _flash_result(acc_sc).reshape(heads, tq, HEAD_DIM)


def _nsa_selected_attention(q, picked, k, v, tq=128, tk=512):
    b, h, s, d = q.shape
    tq, tk = min(tq, s), min(tk, s)
    n_sel = picked.shape[-1]
    expand = (np.arange(n_sel)[:, None] == np.arange(s)[None, :] // SEL_BLOCK).astype(np.float32)
    expand = jnp.asarray(expand, dtype=BF16)
    kern = functools.partial(_nsa_sel_kernel, tq=tq, tk=tk, heads=h)
    return pl.pallas_call(
        kern,
        grid=(b, s // tq),
        in_specs=[pl.BlockSpec((1, h, tq, d), lambda i, t: (i, 0, t, 0)),
                  pl.BlockSpec((1, tq, n_sel), lambda i, t: (i, t, 0)),
                  pl.BlockSpec((n_sel, s), lambda i, t: (0, 0)),
                  pl.BlockSpec((1, d, s), lambda i, t: (i, 0, 0)),
                  pl.BlockSpec((1, s, LANES), lambda i, t: (i, 0, 0))],
        out_specs=pl.BlockSpec((1, h, tq, d), lambda i, t: (i, 0, t, 0)),
        out_shape=jax.ShapeDtypeStruct((b, h, s, d), F32),
        scratch_shapes=[pltpu.VMEM((h * tq, LANES), F32), pltpu.VMEM((h * tq, LANES), F32)],
        compiler_params=_params("parallel", "arbitrary"),
        name="nsa_selected_attention",
    )(q, picked, expand, _keys_major(k), _augment_values(v))


def _nsa_win_kernel(q_ref, kt_ref, v_ref, oc_ref, os_ref, g_ref, o_ref, m_sc, acc_sc, *, tq, tk, heads):
    q0 = pl.program_id(1) * tq
    j_lo = jnp.maximum(q0 - WINDOW, 0) // tk
    j_hi = (q0 + tq) // tk
    q = q_ref[0].reshape(heads * tq, HEAD_DIM)
    _flash_init(m_sc, acc_sc)

    def tile(j, c):
        k0 = pl.multiple_of(j * tk, tk)
        qpos, kpos = _tile_positions(q0, k0, tq, tk)
        mask = _stack_mask((kpos <= qpos) & (kpos > qpos - WINDOW), heads)
        s = jnp.dot(q, kt_ref[0, :, pl.ds(k0, tk)], preferred_element_type=F32)
        _flash_step(jnp.where(mask, s, NEG_INF), v_ref[0, pl.ds(k0, tk), :], m_sc, acc_sc)
        return c

    lax.fori_loop(j_lo, j_hi, tile, 0)
    o_w = _flash_result(acc_sc).reshape(heads, tq, HEAD_DIM)
    g = g_ref[0]
    o_ref[0] = g[:, :, 0:1] * oc_ref[0] + g[:, :, 1:2] * os_ref[0] + g[:, :, 2:3] * o_w


def _nsa_window_merge(q, k, v, o_c, o_s, gates, tq=256, tk=256):
    b, h, s, d = q.shape
    tq, tk = min(tq, s), min(tk, s)
    assert tq % tk == 0 and WINDOW % tk == 0
    kern = functools.partial(_nsa_win_kernel, tq=tq, tk=tk, heads=h)
    head_spec = pl.BlockSpec((1, h, tq, d), lambda i, t: (i, 0, t, 0))
    return pl.pallas_call(
        kern,
        grid=(b, s // tq),
        in_specs=[head_spec,
                  pl.BlockSpec((1, d, s), lambda i, t: (i, 0, 0)),
                  pl.BlockSpec((1, s, LANES), lambda i, t: (i, 0, 0)),
                  head_spec, head_spec,
                  pl.BlockSpec((1, h, tq, 3), lambda i, t: (i, 0, t, 0))],
        out_specs=head_spec,
        out_shape=jax.ShapeDtypeStruct((b, h, s, d), F32),
        scratch_shapes=[pltpu.VMEM((h * tq, LANES), F32), pltpu.VMEM((h * tq, LANES), F32)],
        compiler_params=_params("parallel", "parallel"),
        name="nsa_window_merge",
    )(q, _keys_major(k), _augment_values(v), o_c, o_s, gates)


BIG = 3.0e38


def _first_max(v, lanef):
    best = jnp.max(v, axis=1, keepdims=True)
    first = jnp.min(jnp.where(v == best, lanef, float(LANES)), axis=1, keepdims=True)
    return best, first, lanef == first


def _router_kernel(x_ref, wr_ref, rb_ref, tri_ref, grp_ref, out_ref, cnt_ref, lrank_ref, wdense_ref, base_ref,
                   cnt_sc):
    @pl.when(pl.program_id(0) == 0)
    def _():
        cnt_sc[...] = jnp.zeros(cnt_sc.shape, F32)

    tm = x_ref.shape[0]
    per_group = N_EXPERTS // N_GROUPS
    scores = jax.nn.sigmoid(jnp.dot(x_ref[...].astype(BF16), wr_ref[...], preferred_element_type=F32))
    lane = lax.broadcasted_iota(jnp.int32, (tm, LANES), 1)
    lanef = lane.astype(F32)
    biased = jnp.where(lane < N_EXPERTS, scores + rb_ref[...], -BIG)

    grp_of_lane = lane // per_group
    gs = jnp.full((tm, LANES), -BIG, F32)
    for g in range(N_GROUPS):
        v = jnp.where(grp_of_lane == g, biased, -BIG)
        m1, _, hit = _first_max(v, lanef)
        m2 = jnp.max(jnp.where(hit, -BIG, v), axis=1, keepdims=True)
        gs = jnp.where(lane == g, m1 + m2, gs)
    gsel = jnp.zeros((tm, LANES), F32)
    for _ in range(TOPK_GROUPS):
        _, _, hit = _first_max(gs, lanef)
        gsel = jnp.where(hit, 1.0, gsel)
        gs = jnp.where(hit, -BIG, gs)
    emask = jnp.dot(gsel.astype(BF16), grp_ref[...], preferred_element_type=F32) > 0.5
    cand = jnp.where(lane < N_EXPERTS, jnp.where(emask, biased, NEG_INF), -BIG)

    picked = jnp.zeros((tm, LANES), F32)
    hits = []
    for _ in range(TOP_K):
        _, first, hit = _first_max(cand, lanef)
        hits.append((first, hit))
        picked = jnp.where(hit, 1.0, picked)
        cand = jnp.where(hit, -BIG, cand)
    w = scores * picked
    w = w / jnp.sum(w, axis=1, keepdims=True) * ROUTE_SCALE

    local_rank = jnp.dot(tri_ref[...], picked.astype(BF16), preferred_element_type=F32)
    rank = local_rank + cnt_sc[...]
    lrank_ref[...] = jnp.where(picked > 0.0, local_rank, -1.0)
    wdense_ref[...] = w
    base_ref[0] = cnt_sc[...]
    cnt_sc[...] += jnp.sum(picked, axis=0, keepdims=True)
    cnt_ref[...] = cnt_sc[...]

    out = jnp.zeros((tm, LANES), F32)
    for k, (first, hit) in enumerate(hits):
        out = jnp.where(lane == k, first, out)
        out = jnp.where(lane == TOP_K + k, jnp.sum(jnp.where(hit, w, 0.0), axis=1, keepdims=True), out)
        out = jnp.where(lane == 2 * TOP_K + k, jnp.sum(jnp.where(hit, rank, 0.0), axis=1, keepdims=True), out)
    out_ref[...] = out


def _router(xt, w_router, router_bias, tm=512):
    n_tok, d = xt.shape
    tm = min(tm, n_tok)
    wr = jnp.pad(w_router.astype(BF16), ((0, 0), (0, LANES - N_EXPERTS)))
    rb = jnp.pad(router_bias.astype(F32), (0, LANES - N_EXPERTS)).reshape(1, LANES)
    tri = jnp.asarray(np.tril(np.ones((tm, tm), np.float32), -1), dtype=BF16)
    per_group = N_EXPERTS // N_GROUPS
    grp = (np.arange(LANES)[:, None] == np.arange(LANES)[None, :] // per_group) & (np.arange(LANES)[None, :] < N_EXPERTS)
    grp = jnp.asarray(grp.astype(np.float32), dtype=BF16)
    n_tiles = n_tok // tm
    out, cnt, local_rank, w_dense, tile_base = pl.pallas_call(
        _router_kernel,
        grid=(n_tok // tm,),
        in_specs=[pl.BlockSpec((tm, d), lambda i: (i, 0)),
                  pl.BlockSpec((d, LANES), lambda i: (0, 0)),
                  pl.BlockSpec((1, LANES), lambda i: (0, 0)),
                  pl.BlockSpec((tm, tm), lambda i: (0, 0)),
                  pl.BlockSpec((LANES, LANES), lambda i: (0, 0))],
        out_specs=[pl.BlockSpec((tm, LANES), lambda i: (i, 0)),
                   pl.BlockSpec((1, LANES), lambda i: (0, 0)),
                   pl.BlockSpec((tm, LANES), lambda i: (i, 0)),
                   pl.BlockSpec((tm, LANES), lambda i: (i, 0)),
                   pl.BlockSpec((1, 1, LANES), lambda i: (i, 0, 0))],
        out_shape=[jax.ShapeDtypeStruct((n_tok, LANES), F32), jax.ShapeDtypeStruct((1, LANES), F32),
                   jax.ShapeDtypeStruct((n_tok, LANES), F32), jax.ShapeDtypeStruct((n_tok, LANES), F32),
                   jax.ShapeDtypeStruct((n_tiles, 1, LANES), F32)],
        scratch_shapes=[pltpu.VMEM((1, LANES), F32)],
        compiler_params=_params("arbitrary"),
        name="moe_router",
    )(xt, wr, rb, tri, grp)
    eidx = out[:, :TOP_K].astype(jnp.int32)
    w = out[:, TOP_K:2 * TOP_K]
    rank = out[:, 2 * TOP_K:3 * TOP_K].astype(jnp.int32)
    tile_base = tile_base[:, 0, :N_EXPERTS].astype(jnp.int32)
    return eidx, w, rank, cnt[0, :N_EXPERTS].astype(jnp.int32), local_rank, w_dense, tile_base


def _expert_kernel(be_ref, x_ref, wg_ref, wu_ref, wd_ref, o_ref):
    del be_ref
    xb = x_ref[...].astype(BF16)
    g = jnp.dot(xb, wg_ref[0].astype(BF16), preferred_element_type=F32)
    u = jnp.dot(xb, wu_ref[0].astype(BF16), preferred_element_type=F32)
    h = (g * jax.nn.sigmoid(g)) * u
    o_ref[...] = jnp.dot(h.astype(BF16), wd_ref[0].astype(BF16), preferred_element_type=F32).astype(o_ref.dtype)


def _expert_blocks(xs, blk_expert, w_g, w_u, w_d):
    n_slots, d = xs.shape
    d_e = w_g.shape[-1]
    n_blk = n_slots // MOE_BLOCK
    return pl.pallas_call(
        _expert_kernel,
        grid_spec=pltpu.PrefetchScalarGridSpec(
            num_scalar_prefetch=1,
            grid=(n_blk,),
            in_specs=[pl.BlockSpec((MOE_BLOCK, d), lambda i, be: (i, 0)),
                      pl.BlockSpec((1, d, d_e), lambda i, be: (be[i], 0, 0)),
                      pl.BlockSpec((1, d, d_e), lambda i, be: (be[i], 0, 0)),
                      pl.BlockSpec((1, d_e, d), lambda i, be: (be[i], 0, 0))],
            out_specs=pl.BlockSpec((MOE_BLOCK, d), lambda i, be: (i, 0))),
        out_shape=jax.ShapeDtypeStruct((n_slots, d), BF16),
        compiler_params=_params("arbitrary"),
        name="routed_experts",
    )(blk_expert.astype(jnp.int32), xs, w_g, w_u, w_d)


COMBINE_WINDOW = 128
COMBINE_GROUP = 4
BF16_ROW_TILE = 16


def _combine_kernel(start_ref, off_ref, extra_ref, lrank_ref, w_ref, y_hbm, o_ref, ybuf, sem, xbuf, xsem, acc_sc):
    i = pl.program_id(0)
    tm, d = o_ref.shape
    win = COMBINE_WINDOW
    col = lax.broadcasted_iota(jnp.int32, (1, win), 1).astype(F32)
    n_groups = N_EXPERTS // COMBINE_GROUP

    def window(buf, h, e):
        start = pl.multiple_of(start_ref[i * N_EXPERTS + e], BF16_ROW_TILE)
        return pltpu.make_async_copy(y_hbm.at[pl.ds(start, win), :], ybuf.at[buf, h], sem.at[buf, h])

    def weights(rank_col, w_col, first_row):
        return jnp.where(rank_col == col + first_row, w_col, 0.0).astype(BF16)

    for h in range(COMBINE_GROUP):
        window(0, h, h).start()
    for g in range(n_groups):
        buf = g % 2
        if g + 1 < n_groups:
            for h in range(COMBINE_GROUP):
                window(1 - buf, h, (g + 1) * COMBINE_GROUP + h).start()
        parts = []
        for h in range(COMBINE_GROUP):
            e = g * COMBINE_GROUP + h
            window(buf, h, e).wait()
            first_row = (-off_ref[i * N_EXPERTS + e]).astype(F32)
            parts.append(weights(lrank_ref[:, e:e + 1], w_ref[:, e:e + 1], first_row))
        term = jnp.dot(jnp.concatenate(parts, axis=1), ybuf[buf].reshape(COMBINE_GROUP * win, d),
                       preferred_element_type=F32)
        if g == 0:
            acc_sc[...] = term
        else:
            acc_sc[...] += term

    lane = lax.broadcasted_iota(jnp.int32, (tm, LANES), 1)

    def extra_chunks(e, c):
        n_extra = extra_ref[i * N_EXPERTS + e]
        rank_col = jnp.sum(jnp.where(lane == e, lrank_ref[...], 0.0), axis=1, keepdims=True)
        w_col = jnp.sum(jnp.where(lane == e, w_ref[...], 0.0), axis=1, keepdims=True)

        def chunk(k, c2):
            start = pl.multiple_of(start_ref[i * N_EXPERTS + e] + (k + 1) * win, BF16_ROW_TILE)
            cp = pltpu.make_async_copy(y_hbm.at[pl.ds(start, win), :], xbuf, xsem)
            cp.start()
            cp.wait()
            first_row = ((k + 1) * win - off_ref[i * N_EXPERTS + e]).astype(F32)
            acc_sc[...] += jnp.dot(weights(rank_col, w_col, first_row), xbuf[...], preferred_element_type=F32)
            return c2

        lax.fori_loop(0, n_extra, chunk, 0)
        return c

    lax.fori_loop(0, jnp.where(extra_ref[N_EXPERTS * pl.num_programs(0) + i] > 0, N_EXPERTS, 0), extra_chunks, 0)
    o_ref[...] = acc_sc[...]


def _combine(y_slots, local_rank, w_dense, tile_base, counts, pad_start, tm=512):
    n_tok = local_rank.shape[0]
    tm = min(tm, n_tok)
    n_tiles = n_tok // tm
    d = y_slots.shape[1]
    win = COMBINE_WINDOW
    first = pad_start[None, :] + tile_base
    length = jnp.concatenate([tile_base[1:], counts[None, :]], axis=0) - tile_base
    start = first // BF16_ROW_TILE * BF16_ROW_TILE
    off = first - start
    extra = jnp.maximum((off + length + win - 1) // win - 1, 0)
    extra = jnp.concatenate([extra.reshape(-1), jnp.sum(extra, axis=1)])
    return pl.pallas_call(
        _combine_kernel,
        grid_spec=pltpu.PrefetchScalarGridSpec(
            num_scalar_prefetch=3,
            grid=(n_tiles,),
            in_specs=[pl.BlockSpec((tm, LANES), lambda i, *_: (i, 0)),
                      pl.BlockSpec((tm, LANES), lambda i, *_: (i, 0)),
                      pl.BlockSpec(memory_space=pl.ANY)],
            out_specs=pl.BlockSpec((tm, d), lambda i, *_: (i, 0)),
            scratch_shapes=[pltpu.VMEM((2, COMBINE_GROUP, win, d), BF16),
                            pltpu.SemaphoreType.DMA((2, COMBINE_GROUP)),
                            pltpu.VMEM((win, d), BF16), pltpu.SemaphoreType.DMA(()),
                            pltpu.VMEM((tm, d), F32)]),
        out_shape=jax.ShapeDtypeStruct((n_tok, d), F32),
        compiler_params=_params("arbitrary"),
        name="moe_combine",
    )(start.reshape(-1).astype(jnp.int32), off.reshape(-1).astype(jnp.int32), extra.astype(jnp.int32),
      local_rank, w_dense, y_slots)


def _layer_norm(x, g, b):
    mu = jnp.mean(x, axis=-1, keepdims=True)
    var = jnp.mean(jnp.square(x - mu), axis=-1, keepdims=True)
    return ((x - mu) * lax.rsqrt(var + LN_EPS)) * g + b


def _mixer_out_kernel(x_ref, oa_ref, ob_ref, oc_ref, od_ref, wg_ref, wbr_ref, wo_ref, g_ref, b_ref, o_ref, *, alpha):
    x = x_ref[...]
    xb = x.astype(BF16)
    merged = None
    for m, branch in enumerate((oa_ref, ob_ref, oc_ref, od_ref)):
        gate = jax.nn.sigmoid(jnp.dot(xb, wg_ref[m], preferred_element_type=F32))
        term = gate * jnp.dot(branch[...].astype(BF16), wbr_ref[m], preferred_element_type=F32)
        merged = term if merged is None else merged + term
    mix = jnp.dot(merged.astype(BF16), wo_ref[...], preferred_element_type=F32)
    o_ref[...] = _layer_norm(alpha * x + mix, g_ref[...], b_ref[...])


def _mixer_out(xt, branches, w_gate, w_br, w_o, ln_g, ln_b, alpha, tm=256):
    n_tok, d = xt.shape
    tm = min(tm, n_tok)
    width = branches[0].shape[1]
    row_spec = pl.BlockSpec((tm, d), lambda i: (i, 0))
    branch_spec = pl.BlockSpec((tm, width), lambda i: (i, 0))
    vec_spec = pl.BlockSpec((1, d), lambda i: (0, 0))
    kern = functools.partial(_mixer_out_kernel, alpha=alpha)
    return pl.pallas_call(
        kern,
        grid=(n_tok // tm,),
        in_specs=[row_spec, branch_spec, branch_spec, branch_spec, branch_spec,
                  pl.BlockSpec((N_MIXERS, d, d), lambda i: (0, 0, 0)),
                  pl.BlockSpec((N_MIXERS, width, d), lambda i: (0, 0, 0)),
                  pl.BlockSpec((d, d), lambda i: (0, 0)),
                  vec_spec, vec_spec],
        out_specs=row_spec,
        out_shape=jax.ShapeDtypeStruct((n_tok, d), F32),
        compiler_params=_params("parallel"),
        name="mixer_merge_out_norm",
    )(xt, *branches, w_gate.astype(BF16), w_br.astype(BF16), w_o.astype(BF16),
      ln_g.reshape(1, d), ln_b.reshape(1, d))


def _moe_out_kernel(x_ref, r_ref, wg_ref, wu_ref, wd_ref, g_ref, b_ref, o_ref, *, alpha):
    x = x_ref[...]
    xb = x.astype(BF16)
    gate = jnp.dot(xb, wg_ref[...], preferred_element_type=F32)
    up = jnp.dot(xb, wu_ref[...], preferred_element_type=F32)
    hidden = (gate * jax.nn.sigmoid(gate)) * up
    shared = jnp.dot(hidden.astype(BF16), wd_ref[...], preferred_element_type=F32)
    o_ref[...] = _layer_norm(alpha * x + (r_ref[...] + shared), g_ref[...], b_ref[...])


def _moe_out(xt, routed, w_sh_gate, w_sh_up, w_sh_down, ln_g, ln_b, alpha, tm=512):
    n_tok, d = xt.shape
    tm = min(tm, n_tok)
    d_e = w_sh_gate.shape[1]
    row_spec = pl.BlockSpec((tm, d), lambda i: (i, 0))
    vec_spec = pl.BlockSpec((1, d), lambda i: (0, 0))
    kern = functools.partial(_moe_out_kernel, alpha=alpha)
    return pl.pallas_call(
        kern,
        grid=(n_tok // tm,),
        in_specs=[row_spec, row_spec,
                  pl.BlockSpec((d, d_e), lambda i: (0, 0)), pl.BlockSpec((d, d_e), lambda i: (0, 0)),
                  pl.BlockSpec((d_e, d), lambda i: (0, 0)), vec_spec, vec_spec],
        out_specs=row_spec,
        out_shape=jax.ShapeDtypeStruct((n_tok, d), F32),
        compiler_params=_params("parallel"),
        name="moe_shared_out_norm",
    )(xt, routed, w_sh_gate.astype(BF16), w_sh_up.astype(BF16), w_sh_down.astype(BF16),
      ln_g.reshape(1, d), ln_b.reshape(1, d))


def _rope(x, positions):
    half = ROPE_DIMS // 2
    inv_freq = jnp.float32(ROPE_THETA) ** (-jnp.arange(half, dtype=F32) / half)
    ang = positions.astype(F32)[..., None] * inv_freq
    shape = ang.shape[:2] + (1,) * (x.ndim - 3) + (half,)
    cos = jnp.cos(ang).reshape(shape)
    sin = jnp.sin(ang).reshape(shape)
    x1, x2, rest = x[..., :half], x[..., half:ROPE_DIMS], x[..., ROPE_DIMS:]
    return jnp.concatenate([x1 * cos - x2 * sin, x1 * sin + x2 * cos, rest], axis=-1)


def _mixer_layer(x, positions, w_in, b_f, pe_cmp, w_cmp1, w_cmp2, w_gate, w_br, w_o, ln_g, ln_b, alpha):
    b_, s_, d_ = x.shape
    xt = x.reshape(b_ * s_, d_)
    proj = _mm(xt, w_in).reshape(b_, s_, IN_WIDTH)
    col = lambda name: proj[..., IN_OFFSETS[name]:IN_OFFSETS[name] + dict(IN_LAYOUT)[name]]
    hd = lambda t: t.reshape(b_, s_, -1, HEAD_DIM)
    heads_first = lambda t: jnp.moveaxis(t, 2, 1)
    as_q = lambda t: (heads_first(t) * ATTN_SCALE).astype(BF16)
    as_kv = lambda t: heads_first(t).astype(BF16)
    merge_heads = lambda o: jnp.moveaxis(o, 1, 2).reshape(b_ * s_, MIX_WIDTH)

    a_q = as_q(_rope(hd(col('a_q')), positions))
    cmp_kv = _compress(jnp.stack([col('a_k_cmp'), col('a_v_cmp')]), pe_cmp, w_cmp1, w_cmp2).astype(BF16)
    o_cmp, picked = _nsa_cmp_select(a_q, cmp_kv[0], cmp_kv[1])
    o_sel = _nsa_selected_attention(a_q, picked, _rope(col('a_k_sel'), positions).astype(BF16),
                                    col('a_v_sel').astype(BF16))
    gates = heads_first(jax.nn.sigmoid(col('a_gate').reshape(b_, s_, HEADS_PER_MIXER, 3)))
    o_a = _nsa_window_merge(a_q, _rope(col('a_k_win'), positions).astype(BF16), col('a_v_win').astype(BF16),
                            o_cmp, o_sel, gates)
    o_b = _stick_breaking_attention(as_q(hd(col('b_q'))), as_kv(hd(col('b_k'))), as_kv(hd(col('b_v'))))
    log_f = jax.nn.log_sigmoid(col('c_f') + b_f)
    cum = jnp.moveaxis(jnp.cumsum(log_f, axis=1), 1, 2)
    o_c = _fox_attention(as_q(hd(col('c_q'))), as_kv(hd(col('c_k'))), as_kv(hd(col('c_v'))), cum)
    d_iq = heads_first(_rope(col('d_iq').reshape(b_, s_, IDX_HEADS, IDX_DIM), positions)).astype(BF16)
    o_d = _dsa_attention(as_q(_rope(hd(col('d_q')), positions)), _rope(col('d_k'), positions).astype(BF16),
                         col('d_v').astype(BF16), d_iq, _rope(col('d_ik'), positions).astype(BF16), col('d_iw'))

    out = _mixer_out(xt, [merge_heads(o) for o in (o_a, o_b, o_c, o_d)], w_gate, w_br, w_o, ln_g, ln_b, alpha)
    return out.reshape(b_, s_, d_)


def _moe_layer(x, w_router, router_bias, w_exp_gate, w_exp_up, w_exp_down, w_sh_gate, w_sh_up, w_sh_down,
               ln_g, ln_b, alpha):
    b_, s_, d_ = x.shape
    n_tok = b_ * s_
    xt = x.reshape(n_tok, d_)
    eidx, _, rank, counts, local_rank, w_dense, tile_base = _router(xt, w_router, router_bias)
    n_assign = n_tok * TOP_K
    padded = (counts + MOE_BLOCK - 1) // MOE_BLOCK * MOE_BLOCK
    pad_end = jnp.cumsum(padded)
    pad_start = pad_end - padded
    dest = pad_start[eidx] + rank
    n_slots = n_assign + (N_EXPERTS + 1) * MOE_BLOCK
    n_blk = n_slots // MOE_BLOCK
    flat_tok = jnp.repeat(jnp.arange(n_tok, dtype=jnp.int32), TOP_K)
    slot_tok = jnp.full((n_slots,), n_tok, jnp.int32).at[dest.reshape(n_assign)].set(flat_tok)
    blk_start = jnp.arange(n_blk, dtype=jnp.int32) * MOE_BLOCK
    blk_expert = jnp.minimum(jnp.sum(pad_end[None, :] <= blk_start[:, None], axis=1), N_EXPERTS - 1)
    x_pad = jnp.concatenate([xt, jnp.zeros((1, d_), xt.dtype)], axis=0)
    xs = x_pad[slot_tok]
    y_slots = _expert_blocks(xs, blk_expert, w_exp_gate, w_exp_up, w_exp_down)
    routed = _combine(y_slots, local_rank, w_dense, tile_base, counts, pad_start)
    return _moe_out(xt, routed, w_sh_gate, w_sh_up, w_sh_down, ln_g, ln_b, alpha).reshape(b_, s_, d_)


def kernel(x, positions, w_in, b_f, pe_cmp, w_cmp1, w_cmp2, w_gate, w_br, w_o, ln1_g, ln1_b, w_router, router_bias, w_exp_gate, w_exp_up, w_exp_down, w_sh_gate, w_sh_up, w_sh_down, ln2_g, ln2_b):
    depth = w_in.shape[0]
    alpha = (2 * depth) ** 0.25
    for l in range(depth):
        x = _mixer_layer(x, positions, w_in[l], b_f[l], pe_cmp[l], w_cmp1[l], w_cmp2[l],
                         w_gate[l], w_br[l], w_o[l], ln1_g[l], ln1_b[l], alpha)
        x = _moe_layer(x, w_router[l], router_bias[l], w_exp_gate[l], w_exp_up[l], w_exp_down[l],
                       w_sh_gate[l], w_sh_up[l], w_sh_down[l], ln2_g[l], ln2_b[l], alpha)
    return x
```
